```python
import math
import jax
import jax.numpy as jnp
from jax import lax
import numpy as np

D_MODEL = 2048
BATCH = 2
SEQ = 16384
DEPTH = 1
DEC_BATCH = 16
DEC_SEQ = 16
PAST_LEN = 2048

CHUNK = 64
H_A = 16
DK_A = 128
DV_A = 128
D_QK_A = H_A * DK_A
D_V_A = H_A * DV_A
CONV_WIDTH = 4
D_CONV_A = 2 * D_QK_A + D_V_A
H_B = 16
DK_B = 128
DV_B = 128
D_K_B = H_B * DK_B
D_V_B = H_B * DV_B
N_BRANCH = 2
D_BRANCH = D_V_A
D_IN = D_CONV_A + 2 * H_A + D_V_A + 2 * D_K_B + 2 * D_V_B + N_BRANCH * D_MODEL
N_EXPERTS = 32
TOP_K = 4
D_FF = 2048
SWIGLU_LIMIT = 7.0
SWIGLU_ALPHA = 1.702
MOE_BLOCK = 128
EPS = 1e-6

kernel_name = 'hybrid_gdn_hgrn2_moe_stream_step'


def rms_norm(x, g):
    xf = x.astype(jnp.float32)
    y = xf * lax.rsqrt(jnp.mean(xf * xf, axis=-1, keepdims=True) + EPS)
    return (y * g.astype(jnp.float32)).astype(x.dtype)


def head_rms_norm(o, g):
    return o * lax.rsqrt(jnp.mean(o * o, axis=-1, keepdims=True) + EPS) * g.astype(jnp.float32)


def l2_normalize(x):
    return x * lax.rsqrt(jnp.sum(x * x, axis=-1, keepdims=True) + EPS)


def causal_short_conv(u, conv_state, conv_w):
    t = u.shape[1]
    full = jnp.concatenate([conv_state.astype(u.dtype), u], axis=1)
    out = full[:, 0:t] * conv_w[0]
    for j in range(1, CONV_WIDTH):
        out = out + full[:, j:j + t] * conv_w[j]
    return jax.nn.silu(out), full[:, t:]


def to_chunks(x, chunk):
    b, t, h = x.shape[:3]
    n = t // chunk
    x = x.reshape((b, n, chunk, h) + x.shape[3:])
    x = jnp.moveaxis(x, 2, 3)
    return jnp.moveaxis(x, 1, 0)


def from_chunks(o):
    n, b, h, c, v = o.shape
    return jnp.transpose(o, (1, 0, 3, 2, 4)).reshape(b, n * c, h, v)


def gated_delta_rule(q, k, v, g, beta, s0, chunk):
    q, k, v = to_chunks(q, chunk), to_chunks(k, chunk), to_chunks(v, chunk)
    g, beta = to_chunks(g, chunk), to_chunks(beta, chunk)
    c = chunk
    causal = jnp.tril(jnp.ones((c, c), dtype=bool))
    strict = jnp.tril(jnp.ones((c, c), dtype=bool), k=-1)
    gc = jnp.cumsum(g, axis=-1)
    diff = gc[..., :, None] - gc[..., None, :]
    decay = jnp.where(causal, jnp.exp(jnp.where(causal, diff, 0.0)), 0.0)
    kb = k * beta[..., None]
    a = jnp.where(strict, jnp.einsum('nbhtk,nbhsk->nbhts', kb, k) * decay, 0.0)
    tmat = a + jnp.eye(c, dtype=a.dtype)
    dv = v.shape[-1]
    rhs = jnp.concatenate([v * beta[..., None], kb * jnp.exp(gc)[..., None]], axis=-1)
    sol = lax.linalg.triangular_solve(tmat, rhs, left_side=True, lower=True, unit_diagonal=True)
    u, w = sol[..., :dv], sol[..., dv:]
    qk = jnp.einsum('nbhtk,nbhsk->nbhts', q, k) * decay
    q_dec = q * jnp.exp(gc)[..., None]
    k_dec = k * jnp.exp(gc[..., -1:] - gc)[..., None]
    g_last = jnp.exp(gc[..., -1])

    def step(s, xs):
        u_c, w_c, qk_c, qd_c, kd_c, gl_c = xs
        v_new = u_c - jnp.einsum('bhtk,bhkv->bhtv', w_c, s)
        o_c = jnp.einsum('bhtk,bhkv->bhtv', qd_c, s) + jnp.einsum('bhts,bhsv->bhtv', qk_c, v_new)
        s = s * gl_c[..., None, None] + jnp.einsum('bhtk,bhtv->bhkv', kd_c, v_new)
        return s, o_c

    s_final, o = lax.scan(step, s0, (u, w, qk, q_dec, k_dec, g_last))
    return from_chunks(o), s_final


def gla_recurrence(q, k, v, log_f, s0, chunk):
    q, k, v, log_f = to_chunks(q, chunk), to_chunks(k, chunk), to_chunks(v, chunk), to_chunks(log_f, chunk)
    c = chunk
    causal = jnp.tril(jnp.ones((c, c), dtype=bool))
    bc = jnp.cumsum(log_f, axis=-2)
    q_dec = q * jnp.exp(bc)
    k_dec = k * jnp.exp(bc[..., -1:, :] - bc)
    f_last = jnp.exp(bc[..., -1, :])

    def step(s, xs):
        q_c, k_c, v_c, b_c, qd_c, kd_c, fl_c = xs
        diff = b_c[..., :, None, :] - b_c[..., None, :, :]
        decay = jnp.exp(jnp.where(causal[:, :, None], diff, -jnp.inf))
        scores = jnp.einsum('bhtk,bhsk,bhtsk->bhts', q_c, k_c, decay)
        o_c = jnp.einsum('bhtk,bhkv->bhtv', qd_c, s) + jnp.einsum('bhts,bhsv->bhtv', scores, v_c)
        s = s * fl_c[..., None] + jnp.einsum('bhsk,bhsv->bhkv', kd_c, v_c)
        return s, o_c

    s_final, o = lax.scan(step, s0, (q, k, v, bc, q_dec, k_dec, f_last))
    return from_chunks(o), s_final


def hybrid_mixer(xn, conv_state, s_delta, s_hgrn, w_in, conv_w, a_log, dt_bias, norm_a, lb, norm_b,
                 w_branch, w_out, chunk):
    b, t, _ = xn.shape
    f32 = jnp.float32
    proj = xn @ w_in
    splits = [int(s) for s in np.cumsum([D_CONV_A, H_A, H_A, D_V_A, D_K_B, D_K_B, D_V_B, D_V_B])]
    qkv_a, beta_l, alpha_l, z_a, q_b, f_b, i_b, z_b, gate_l = jnp.split(proj, splits, axis=-1)
    qkv_c, new_conv = causal_short_conv(qkv_a, conv_state, conv_w)
    qkv_c = qkv_c.astype(f32)
    q_a = l2_normalize(qkv_c[..., :D_QK_A].reshape(b, t, H_A, DK_A)) * (DK_A ** -0.5)
    k_a = l2_normalize(qkv_c[..., D_QK_A:2 * D_QK_A].reshape(b, t, H_A, DK_A))
    v_a = qkv_c[..., 2 * D_QK_A:].reshape(b, t, H_A, DV_A)
    beta = jax.nn.sigmoid(beta_l.astype(f32))
    g = -jnp.exp(a_log.astype(f32)) * jax.nn.softplus(alpha_l.astype(f32) + dt_bias.astype(f32))
    o_a, s_delta_new = gated_delta_rule(q_a, k_a, v_a, g, beta, s_delta.astype(f32), chunk)
    o_a = head_rms_norm(o_a, norm_a).reshape(b, t, D_V_A) * jax.nn.silu(z_a.astype(f32))
    lb_h = lb.astype(f32).reshape(H_B, DK_B)
    f_l = f_b.astype(f32).reshape(b, t, H_B, DK_B)
    log_f = jnp.log(lb_h + (1.0 - lb_h) * jax.nn.sigmoid(f_l))
    k_b = (1.0 - lb_h) * jax.nn.sigmoid(-f_l)
    q_bh = jax.nn.silu(q_b.astype(f32)).reshape(b, t, H_B, DK_B)
    i_bh = i_b.astype(f32).reshape(b, t, H_B, DV_B)
    o_b, s_hgrn_new = gla_recurrence(q_bh, k_b, i_bh, log_f, s_hgrn.astype(f32), chunk)
    o_b = head_rms_norm(o_b, norm_b).reshape(b, t, D_V_B) * jax.nn.sigmoid(z_b.astype(f32))
    branches = jnp.stack([o_a, o_b], axis=2).astype(xn.dtype)
    u = jnp.einsum('btnw,nwd->btnd', branches, w_branch)
    gates = jax.nn.sigmoid(gate_l.astype(f32).reshape(b, t, N_BRANCH, D_MODEL))
    merged = jnp.sum(gates * u.astype(f32), axis=2).astype(xn.dtype)
    out = merged @ w_out
    return out, new_conv, s_delta_new.astype(s_delta.dtype), s_hgrn_new.astype(s_hgrn.dtype)


def clamped_swiglu(gu):
    gate, up = gu[..., :D_FF], gu[..., D_FF:]
    gate = jnp.minimum(gate, SWIGLU_LIMIT)
    up = jnp.clip(up, -SWIGLU_LIMIT, SWIGLU_LIMIT)
    return (up + 1.0) * gate * jax.nn.sigmoid(SWIGLU_ALPHA * gate)


def moe_ffn(h, w_router, b_router, w_gate_up, b_gate_up, w_down, b_down):
    b, t, d = h.shape
    f32 = jnp.float32
    xt = h.reshape(-1, d)
    n_tok = xt.shape[0]
    logits = (xt @ w_router).astype(f32) + b_router.astype(f32)
    top_logit, top_idx = lax.top_k(logits, TOP_K)
    gate = jax.nn.softmax(top_logit, axis=-1)
    n_assign = n_tok * TOP_K
    e_flat = top_idx.reshape(-1)
    order = jnp.argsort(e_flat)
    e_sorted = e_flat[order]
    tok_sorted = (order // TOP_K).astype(jnp.int32)
    gate_sorted = gate.reshape(-1)[order]
    counts = jnp.bincount(e_flat, length=N_EXPERTS)
    padded = ((counts + MOE_BLOCK - 1) // MOE_BLOCK) * MOE_BLOCK
    start = jnp.cumsum(counts) - counts
    pend = jnp.cumsum(padded)
    pstart = pend - padded
    dest = pstart[e_sorted] + (jnp.arange(n_assign) - start[e_sorted])
    n_blocks = -(-n_assign // MOE_BLOCK) + N_EXPERTS
    n_rows = n_blocks * MOE_BLOCK
    row_tok = jnp.zeros((n_rows,), jnp.int32).at[dest].set(tok_sorted)
    row_gate = jnp.zeros((n_rows,), f32).at[dest].set(gate_sorted)
    block_expert = jnp.minimum(
        jnp.searchsorted(pend, jnp.arange(n_blocks) * MOE_BLOCK, side='right'), N_EXPERTS - 1)

    def block_step(y, xs):
        toks, gts, e = xs
        xb = xt[toks]
        gu = (xb @ w_gate_up[e] + b_gate_up[e]).astype(f32)
        act = clamped_swiglu(gu).astype(xt.dtype)
        out = (act @ w_down[e] + b_down[e]).astype(f32)
        return y.at[toks].add(out * gts[:, None]), None

    y0 = jnp.zeros((n_tok, d), f32)
    y, _ = lax.scan(block_step, y0, (row_tok.reshape(n_blocks, MOE_BLOCK),
                                     row_gate.reshape(n_blocks, MOE_BLOCK), block_expert))
    return y.astype(h.dtype).reshape(b, t, d)


def encoder_trunk(x, conv_states, delta_states, hgrn_states, norm_mix, w_in, conv_a, a_log, dt_bias,
                  norm_a, lb_logits, norm_b, w_branch, w_out, norm_ffn, w_router, b_router, w_gate_up,
                  b_gate_up, w_down, b_down, norm_final):
    chunk = min(CHUNK, x.shape[1])
    lower_bounds = jnp.cumsum(jax.nn.softmax(lb_logits.astype(jnp.float32), axis=0), axis=0)
    convs, deltas, hgrns = [], [], []
    for l in range(DEPTH):
        mix, c_new, d_new, h_new = hybrid_mixer(
            rms_norm(x, norm_mix[l]), conv_states[l], delta_states[l], hgrn_states[l], w_in[l], conv_a[l],
            a_log[l], dt_bias[l], norm_a[l], lower_bounds[l], norm_b[l], w_branch[l], w_out[l], chunk)
        x = x + mix
        x = x + moe_ffn(rms_norm(x, norm_ffn[l]), w_router[l], b_router[l], w_gate_up[l], b_gate_up[l],
                        w_down[l], b_down[l])
        convs.append(c_new)
        deltas.append(d_new)
        hgrns.append(h_new)
    return rms_norm(x, norm_final), jnp.stack(convs), jnp.stack(deltas), jnp.stack(hgrns)


def setup_inputs(seed: int = 0) -> dict:
    key = jax.random.key(seed)
    ks = jax.random.split(key, 24)
    f32 = jnp.float32

    def nrm(k, shape, s):
        return jax.random.normal(k, shape, f32) * s

    dt = jnp.exp(jax.random.uniform(ks[8], (DEPTH, H_A), f32, math.log(1e-3), math.log(1e-1)))
    return {
        'x_prompt': nrm(ks[0], (BATCH, SEQ, D_MODEL), 1.0),
        'x_sample': nrm(ks[1], (DEC_BATCH, DEC_SEQ, D_MODEL), 1.0),
        'state_conv_a': nrm(ks[2], (DEPTH, DEC_BATCH, CONV_WIDTH - 1, D_CONV_A), 1.0),
        'state_delta': nrm(ks[3], (DEPTH, DEC_BATCH, H_A, DK_A, DV_A), 0.1),
        'state_hgrn': nrm(ks[4], (DEPTH, DEC_BATCH, H_B, DK_B, DV_B), 0.5),
        'norm_mix': 1.0 + nrm(ks[5], (DEPTH, D_MODEL), 0.02),
        'w_in': nrm(ks[6], (DEPTH, D_MODEL, D_IN), D_MODEL ** -0.5),
        'conv_a': nrm(ks[7], (DEPTH, CONV_WIDTH, D_CONV_A), CONV_WIDTH ** -0.5),
        'a_log': jnp.log(jax.random.uniform(ks[9], (DEPTH, H_A), f32, 1.0, 16.0)),
        'dt_bias': dt + jnp.log(-jnp.expm1(-dt)),
        'norm_a': 1.0 + nrm(ks[10], (DEPTH, DV_A), 0.02),
        'lb_logits': nrm(ks[11], (DEPTH + 1, D_K_B), 0.1),
        'norm_b': 1.0 + nrm(ks[12], (DEPTH, DV_B), 0.02),
        'w_branch': nrm(ks[13], (DEPTH, N_BRANCH, D_BRANCH, D_MODEL), D_BRANCH ** -0.5),
        'w_out': nrm(ks[14], (DEPTH, D_MODEL, D_MODEL), D_MODEL ** -0.5),
        'norm_ffn': 1.0 + nrm(ks[15], (DEPTH, D_MODEL), 0.02),
        'w_router': nrm(ks[16], (DEPTH, D_MODEL, N_EXPERTS), D_MODEL ** -0.5),
        'b_router': nrm(ks[17], (DEPTH, N_EXPERTS), 0.01),
        'w_gate_up': nrm(ks[18], (DEPTH, N_EXPERTS, D_MODEL, 2 * D_FF), D_MODEL ** -0.5),
        'b_gate_up': nrm(ks[19], (DEPTH, N_EXPERTS, 2 * D_FF), 0.01),
        'w_down': nrm(ks[20], (DEPTH, N_EXPERTS, D_FF, D_MODEL), D_FF ** -0.5),
        'b_down': nrm(ks[21], (DEPTH, N_EXPERTS, D_MODEL), 0.01),
        'norm_final': 1.0 + nrm(ks[22], (D_MODEL,), 0.02),
    }


def reference(x_prompt, x_sample, state_conv_a, state_delta, state_hgrn, norm_mix, w_in, conv_a, a_log,
              dt_bias, norm_a, lb_logits, norm_b, w_branch, w_out, norm_ffn, w_router, b_router, w_gate_up,
              b_gate_up, w_down, b_down, norm_final):
    weights = (norm_mix, w_in, conv_a, a_log, dt_bias, norm_a, lb_logits, norm_b, w_branch, w_out,
               norm_ffn, w_router, b_router, w_gate_up, b_gate_up, w_down, b_down, norm_final)
    bp = x_prompt.shape[0]
    zero_conv = jnp.zeros((DEPTH, bp, CONV_WIDTH - 1, D_CONV_A), x_prompt.dtype)
    zero_delta = jnp.zeros((DEPTH, bp, H_A, DK_A, DV_A), x_prompt.dtype)
    zero_hgrn = jnp.zeros((DEPTH, bp, H_B, DK_B, DV_B), x_prompt.dtype)
    y_prompt, conv_p, delta_p, hgrn_p = encoder_trunk(x_prompt, zero_conv, zero_delta, zero_hgrn, *weights)
    y_sample, conv_s, delta_s, hgrn_s = encoder_trunk(x_sample, state_conv_a, state_delta, state_hgrn,
                                                      *weights)
    return (y_prompt, y_sample, conv_p, delta_p, hgrn_p, conv_s, delta_s, hgrn_s)
```

```python
import functools

import jax
import jax.numpy as jnp
from jax import lax
from jax.experimental import pallas as pl
from jax.experimental.pallas import tpu as pltpu

F32 = jnp.float32
BF16 = jnp.bfloat16
HIGHEST = lax.Precision.HIGHEST

EPS = 1e-6
HEAD_DIM = 128
N_HEADS = 16
D_HEADS = N_HEADS * HEAD_DIM
CONV_WIDTH = 4
CONV_PAD = 8
BLOCK_ROWS = 256
GLA_CHUNK = 64
GLA_SUB = 16
TOP_K = 4
SWIGLU_LIMIT = 7.0
SWIGLU_ALPHA = 1.702
MOE_ROWS = 512
MOE_FF_TILE = 512
VMEM_LIMIT = 56 * 1024 * 1024

NT_DIMS = (((1,), (1,)), ((), ()))
TN_DIMS = (((0,), (0,)), ((), ()))


def _sigmoid(x):
    return 1.0 / (1.0 + jnp.exp(-x))


def _bdot(a, b):
    return jnp.dot(a.astype(BF16), b.astype(BF16), preferred_element_type=F32)


def _bdot_general(a, b, dims):
    return lax.dot_general(a.astype(BF16), b.astype(BF16), dims, preferred_element_type=F32)


def _hdot(a, b):
    return jnp.dot(a, b, precision=HIGHEST, preferred_element_type=F32)


def _inproj_body(x_ref, g_ref, w_ref, o_ref, xn_ref):
    @pl.when(pl.program_id(1) == 0)
    def _():
        x = x_ref[...]
        ms = jnp.mean(x * x, axis=-1, keepdims=True)
        xn_ref[...] = (x * lax.rsqrt(ms + EPS) * g_ref[...]).astype(BF16)

    o_ref[...] = jnp.dot(xn_ref[...], w_ref[...], preferred_element_type=F32)


def _inproj(x, gain, w, tm, tn):
    n, d = x.shape
    nc = w.shape[1]
    return pl.pallas_call(
        _inproj_body,
        grid=(n // tm, nc // tn),
        in_specs=[pl.BlockSpec((tm, d), lambda i, j: (i, 0)),
                  pl.BlockSpec((1, d), lambda i, j: (0, 0)),
                  pl.BlockSpec((d, tn), lambda i, j: (0, j))],
        out_specs=pl.BlockSpec((tm, tn), lambda i, j: (i, j)),
        out_shape=jax.ShapeDtypeStruct((n, nc), F32),
        scratch_shapes=[pltpu.VMEM((tm, d), BF16)],
        compiler_params=pltpu.CompilerParams(
            dimension_semantics=("parallel", "arbitrary"), vmem_limit_bytes=VMEM_LIMIT),
        name="inproj",
    )(x, gain, w)


def _pad_rows(x, rows):
    if x.shape[0] == rows:
        return x
    return jnp.concatenate([x, jnp.zeros((rows - x.shape[0],) + x.shape[1:], x.dtype)], axis=0)


def _gdn_body(qkv_ref, z_ref, ba_ref, cs_ref, s0_ref, cw_ref, alog_ref, dtb_ref, na_ref,
              o_ref, sout_ref, xbuf, s_scr, *, tin, nt):
    R = BLOCK_ROWS
    t = pl.program_id(1)

    @pl.when(t == 0)
    def _():
        xbuf[0:CONV_PAD, :] = cs_ref[0]
        s_scr[...] = s0_ref[0]

    if nt > 1:
        @pl.when(t > 0)
        def _():
            xbuf[0:CONV_PAD, :] = xbuf[tin:tin + CONV_PAD, :]

    xbuf[CONV_PAD:CONV_PAD + tin, :] = qkv_ref[...]
    if tin < R:
        xbuf[CONV_PAD + tin:CONV_PAD + R, :] = jnp.zeros((R - tin, xbuf.shape[1]), F32)

    rows = lax.broadcasted_iota(jnp.int32, (R, HEAD_DIM), 0)
    lanes = lax.broadcasted_iota(jnp.int32, (R, HEAD_DIM), 1)
    valid = rows < tin
    ti = lax.broadcasted_iota(jnp.int32, (R, R), 0)
    si = lax.broadcasted_iota(jnp.int32, (R, R), 1)
    causal = ti >= si
    merge_key = jnp.where(ti > si, ti ^ si, 0)

    ba = _pad_rows(ba_ref[...], R)
    sp_in = ba + dtb_ref[...]
    softplus = jnp.maximum(sp_in, 0.0) + jnp.log(1.0 + jnp.exp(-jnp.abs(sp_in)))
    g_all = jnp.where(valid, -jnp.exp(alog_ref[...]) * softplus, 0.0)
    beta_all = jnp.where(valid, _sigmoid(ba), 0.0)
    gcum_all = _hdot(causal.astype(F32), g_all)
    gate_src = jnp.where(lanes < N_HEADS, beta_all, gcum_all)

    sel_l = lax.broadcasted_iota(jnp.int32, (HEAD_DIM, 2 * HEAD_DIM), 0)
    sel_c = lax.broadcasted_iota(jnp.int32, (HEAD_DIM, 2 * HEAD_DIM), 1)
    lane0 = (lanes == 0).astype(F32)

    def head(h, carry):
        off = h * HEAD_DIM

        def conv_silu(base):
            cols = pl.ds(pl.multiple_of(base + off, HEAD_DIM), HEAD_DIM)
            acc = None
            for j in range(CONV_WIDTH):
                term = xbuf[pl.ds(CONV_PAD - (CONV_WIDTH - 1) + j, R), cols] * cw_ref[j:j + 1, cols]
                acc = term if acc is None else acc + term
            return acc * _sigmoid(acc)

        qc = conv_silu(0)
        kc = conv_silu(D_HEADS)
        vc = conv_silu(2 * D_HEADS)
        qn = qc * lax.rsqrt(jnp.sum(qc * qc, axis=-1, keepdims=True) + EPS) * (HEAD_DIM ** -0.5)
        kn = jnp.where(valid, kc * lax.rsqrt(jnp.sum(kc * kc, axis=-1, keepdims=True) + EPS), 0.0)

        sel = (((sel_c < HEAD_DIM) & (sel_l == h)) |
               ((sel_c >= HEAD_DIM) & (sel_l == h + N_HEADS))).astype(F32)
        bg = _hdot(gate_src, sel)
        beta = bg[:, :HEAD_DIM]
        gc = bg[:, HEAD_DIM:]
        gc_last = gc[R - 1:R, :]
        gc_t = jnp.concatenate([gc, gc], axis=1)
        gc_s = lax.dot_general(lane0, gc, NT_DIMS, precision=HIGHEST, preferred_element_type=F32)
        decay = jnp.where(causal, jnp.exp(jnp.where(causal, gc_t - gc_s, 0.0)), 0.0)

        kb = kn * beta
        aq = _bdot_general(jnp.concatenate([kb, qn], axis=0), kn, NT_DIMS)
        a_mat = jnp.where(ti > si, aq[:R] * decay, 0.0)
        qk = aq[R:] * decay

        n_mat = None
        b = 1
        while b < R:
            c_m = jnp.where((merge_key >= b) & (merge_key < 2 * b), a_mat, 0.0)
            if n_mat is None:
                n_mat = -c_m
            else:
                p = c_m + _bdot(n_mat, c_m)
                n_mat = n_mat - p - _bdot(p, n_mat)
            b *= 2

        rhs = jnp.concatenate([vc * beta, kb * jnp.exp(gc)], axis=1)
        sol = rhs + _bdot(n_mat, rhs)
        u = sol[:, :HEAD_DIM]
        w = sol[:, HEAD_DIM:]

        s_old = s_scr[h]
        qd = qn * jnp.exp(gc)
        kd = kn * jnp.exp(gc_last - gc)
        ws = _bdot(jnp.concatenate([w, qd], axis=0), s_old)
        v_new = u - ws[:R]
        o = ws[R:] + _bdot(qk, v_new)
        s_scr[h] = s_old * jnp.exp(gc_last) + _bdot_general(kd, v_new, TN_DIMS)

        cols = pl.ds(pl.multiple_of(off, HEAD_DIM), HEAD_DIM)
        on = o * lax.rsqrt(jnp.mean(o * o, axis=-1, keepdims=True) + EPS) * na_ref[...]
        z = z_ref[:, cols]
        o_ref[:, cols] = (on[:tin] * (z * _sigmoid(z))).astype(BF16)
        return carry

    lax.fori_loop(0, N_HEADS, head, 0)

    @pl.when(t == nt - 1)
    def _():
        sout_ref[0] = s_scr[...]


def _gdn(proj, ba, conv_state, s0, conv_w, alog_l, dtb_l, norm_a, *, batch, tin, nt, row0):
    rb0 = row0 // tin
    body = functools.partial(_gdn_body, tin=tin, nt=nt)
    return pl.pallas_call(
        body,
        grid=(batch, nt),
        in_specs=[
            pl.BlockSpec((tin, 3 * D_HEADS), lambda b, t: (rb0 + b * nt + t, 0)),
            pl.BlockSpec((tin, D_HEADS), lambda b, t: (rb0 + b * nt + t, 3)),
            pl.BlockSpec((tin, HEAD_DIM), lambda b, t: (rb0 + b * nt + t, 0)),
            pl.BlockSpec((1, CONV_PAD, 3 * D_HEADS), lambda b, t: (b, 0, 0)),
            pl.BlockSpec((1, N_HEADS, HEAD_DIM, HEAD_DIM), lambda b, t: (b, 0, 0, 0)),
            pl.BlockSpec((CONV_PAD, 3 * D_HEADS), lambda b, t: (0, 0)),
            pl.BlockSpec((1, HEAD_DIM), lambda b, t: (0, 0)),
            pl.BlockSpec((1, HEAD_DIM), lambda b, t: (0, 0)),
            pl.BlockSpec((1, HEAD_DIM), lambda b, t: (0, 0)),
        ],
        out_specs=[
            pl.BlockSpec((tin, D_HEADS), lambda b, t: (b * nt + t, 0)),
            pl.BlockSpec((1, N_HEADS, HEAD_DIM, HEAD_DIM), lambda b, t: (b, 0, 0, 0)),
        ],
        out_shape=[jax.ShapeDtypeStruct((batch * nt * tin, D_HEADS), BF16),
                   jax.ShapeDtypeStruct((batch, N_HEADS, HEAD_DIM, HEAD_DIM), F32)],
        scratch_shapes=[pltpu.VMEM((BLOCK_ROWS + CONV_PAD, 3 * D_HEADS), F32),
                        pltpu.VMEM((N_HEADS, HEAD_DIM, HEAD_DIM), F32)],
        compiler_params=pltpu.CompilerParams(
            dimension_semantics=("parallel", "arbitrary"), vmem_limit_bytes=VMEM_LIMIT),
        name="gated_delta",
    )(proj, proj, ba, conv_state, s0, conv_w, alog_l, dtb_l, norm_a)


def _gla_body(q_ref, f_ref, i_ref, z_ref, lbl_ref, nb_ref, s0_ref, o_ref, sout_ref,
              s_scr, q_scr, k_scr, v_scr, b_scr, od_scr, *, tin, nt, layer):
    R = BLOCK_ROWS
    n_chunks = R // GLA_CHUNK
    n_sub = GLA_CHUNK // GLA_SUB
    t = pl.program_id(1)

    @pl.when(t == 0)
    def _():
        s_scr[...] = s0_ref[0]

    rows = lax.broadcasted_iota(jnp.int32, (R, HEAD_DIM), 0)
    valid = rows < tin
    sub_idx = (rows & (GLA_CHUNK - 1)) // GLA_SUB
    ti = lax.broadcasted_iota(jnp.int32, (R, R), 0)
    si = lax.broadcasted_iota(jnp.int32, (R, R), 1)
    same_chunk = (ti // GLA_CHUNK) == (si // GLA_CHUNK)
    chunk_tri = ((ti >= si) & same_chunk).astype(F32)
    row_local = lax.broadcasted_iota(jnp.int32, (GLA_SUB, HEAD_DIM), 0)

    def head(h, carry):
        cols = pl.ds(pl.multiple_of(h * HEAD_DIM, HEAD_DIM), HEAD_DIM)
        lbl = lbl_ref[:, cols]
        lb_e = jnp.exp(lbl - jnp.max(lbl, axis=0, keepdims=True))
        lb = jnp.sum(lb_e[0:layer + 1], axis=0, keepdims=True) / jnp.sum(lb_e, axis=0, keepdims=True)

        fl = _pad_rows(f_ref[:, cols], R)
        qx = _pad_rows(q_ref[:, cols], R)
        vv = _pad_rows(i_ref[:, cols], R)
        log_f = jnp.where(valid, jnp.log(lb + (1.0 - lb) * _sigmoid(fl)), 0.0)
        kk = jnp.where(valid, (1.0 - lb) * _sigmoid(-fl), 0.0)
        qq = qx * _sigmoid(qx)
        bc = _hdot(chunk_tri, log_f)
        q_scr[...] = qq
        k_scr[...] = kk
        v_scr[...] = vv
        b_scr[...] = bc

        def chunk_row(row_in_chunk):
            return jnp.concatenate(
                [jnp.broadcast_to(b_scr[c * GLA_CHUNK + row_in_chunk:c * GLA_CHUNK + row_in_chunk + 1, :],
                                  (GLA_CHUNK, HEAD_DIM)) for c in range(n_chunks)], axis=0)

        b_last = chunk_row(GLA_CHUNK - 1)
        qd = qq * jnp.exp(bc)
        kd = kk * jnp.exp(b_last - bc)

        q_fac, k_fac = [], []
        for j in range(n_sub - 1):
            b_ref_row = chunk_row(GLA_SUB * (j + 1) - 1)
            q_fac.append(jnp.where(sub_idx > j, qq * jnp.exp(jnp.minimum(bc - b_ref_row, 0.0)), 0.0))
            k_fac.append(jnp.where(sub_idx == j, kk * jnp.exp(jnp.minimum(b_ref_row - bc, 0.0)), 0.0))
        scores = _bdot_general(jnp.concatenate(q_fac, axis=1), jnp.concatenate(k_fac, axis=1), NT_DIMS)
        o_off = _bdot(jnp.where(same_chunk, scores, 0.0), vv)

        def diag_block(m, c2):
            r0 = pl.multiple_of(m * GLA_SUB, GLA_SUB)
            q_blk = q_scr[pl.ds(r0, GLA_SUB), :]
            b_blk = b_scr[pl.ds(r0, GLA_SUB), :]
            acc = jnp.zeros((GLA_SUB, HEAD_DIM), F32)
            for s in range(GLA_SUB):
                b_s = b_scr[pl.ds(r0 + s, 1), :]
                k_s = k_scr[pl.ds(r0 + s, 1), :]
                v_s = v_scr[pl.ds(r0 + s, 1), :]
                e = jnp.where(row_local >= s, jnp.exp(jnp.minimum(b_blk - b_s, 0.0)), 0.0)
                col = jnp.sum(q_blk * k_s * e, axis=-1, keepdims=True)
                acc = acc + col * v_s
            od_scr[pl.ds(r0, GLA_SUB), :] = acc
            return c2

        lax.fori_loop(0, R // GLA_SUB, diag_block, 0)

        s_t = s_scr[h]
        o_inter = []
        for c in range(n_chunks):
            sl = slice(c * GLA_CHUNK, (c + 1) * GLA_CHUNK)
            o_inter.append(_bdot_general(qd[sl], s_t, NT_DIMS))
            f_last = jnp.exp(b_scr[(c + 1) * GLA_CHUNK - 1:(c + 1) * GLA_CHUNK, :])
            s_t = s_t * f_last + _bdot_general(vv[sl], kd[sl], TN_DIMS)
        s_scr[h] = s_t

        o = jnp.concatenate(o_inter, axis=0) + o_off + od_scr[...]
        on = o * lax.rsqrt(jnp.mean(o * o, axis=-1, keepdims=True) + EPS) * nb_ref[...]
        z = z_ref[:, cols]
        o_ref[:, cols] = (on[:tin] * _sigmoid(z)).astype(BF16)
        return carry

    lax.fori_loop(0, N_HEADS, head, 0)

    @pl.when(t == nt - 1)
    def _():
        sout_ref[0] = s_scr[...]


def _gla(proj, lb_logits, norm_b, s0_t, *, batch, tin, nt, row0, layer):
    rb0 = row0 // tin
    body = functools.partial(_gla_body, tin=tin, nt=nt, layer=layer)
    col_spec = lambda c: pl.BlockSpec((tin, D_HEADS), lambda b, t: (rb0 + b * nt + t, c))
    head_rows = pltpu.VMEM((BLOCK_ROWS, HEAD_DIM), F32)
    return pl.pallas_call(
        body,
        grid=(batch, nt),
        in_specs=[
            col_spec(4), col_spec(5), col_spec(6), col_spec(7),
            pl.BlockSpec(lb_logits.shape, lambda b, t: (0, 0)),
            pl.BlockSpec((1, HEAD_DIM), lambda b, t: (0, 0)),
            pl.BlockSpec((1, N_HEADS, HEAD_DIM, HEAD_DIM), lambda b, t: (b, 0, 0, 0)),
        ],
        out_specs=[
            pl.BlockSpec((tin, D_HEADS), lambda b, t: (b * nt + t, 0)),
            pl.BlockSpec((1, N_HEADS, HEAD_DIM, HEAD_DIM), lambda b, t: (b, 0, 0, 0)),
        ],
        out_shape=[jax.ShapeDtypeStruct((batch * nt * tin, D_HEADS), BF16),
                   jax.ShapeDtypeStruct((batch, N_HEADS, HEAD_DIM, HEAD_DIM), F32)],
        scratch_shapes=[pltpu.VMEM((N_HEADS, HEAD_DIM, HEAD_DIM), F32),
                        head_rows, head_rows, head_rows, head_rows, head_rows],
        compiler_params=pltpu.CompilerParams(
            dimension_semantics=("parallel", "arbitrary"), vmem_limit_bytes=VMEM_LIMIT),
        name="hgrn2",
    )(proj, proj, proj, proj, lb_logits, norm_b, s0_t)


def _merge_body(oa_ref, ob_ref, g0_ref, g1_ref, x_ref, wb0_ref, wb1_ref, wo_ref, nf_ref, wr_ref, br_ref,
                x1_ref, h_ref, lg_ref):
    ua = jnp.dot(oa_ref[...], wb0_ref[...], preferred_element_type=F32)
    ub = jnp.dot(ob_ref[...], wb1_ref[...], preferred_element_type=F32)
    merged = _sigmoid(g0_ref[...]) * ua + _sigmoid(g1_ref[...]) * ub
    x1 = x_ref[...] + jnp.dot(merged.astype(BF16), wo_ref[...], preferred_element_type=F32)
    x1_ref[...] = x1
    hn = x1 * lax.rsqrt(jnp.mean(x1 * x1, axis=-1, keepdims=True) + EPS) * nf_ref[...]
    h_ref[...] = hn.astype(BF16)
    lg_ref[...] = _hdot(hn, wr_ref[...]) + br_ref[...]


def _merge(o_a, o_b, proj, x, wb0, wb1, wo, norm_ffn, w_router, b_router, tm):
    n, d = x.shape
    const = lambda shape: pl.BlockSpec(shape, lambda i: (0, 0), pipeline_mode=pl.Buffered(1))
    return pl.pallas_call(
        _merge_body,
        grid=(n // tm,),
        in_specs=[
            pl.BlockSpec((tm, D_HEADS), lambda i: (i, 0)),
            pl.BlockSpec((tm, D_HEADS), lambda i: (i, 0)),
            pl.BlockSpec((tm, d), lambda i: (i, 8)),
            pl.BlockSpec((tm, d), lambda i: (i, 9)),
            pl.BlockSpec((tm, d), lambda i: (i, 0)),
            const(wb0.shape), const(wb1.shape), const(wo.shape),
            const((1, d)), const(w_router.shape), const((1, HEAD_DIM)),
        ],
        out_specs=[pl.BlockSpec((tm, d), lambda i: (i, 0)),
                   pl.BlockSpec((tm, d), lambda i: (i, 0)),
                   pl.BlockSpec((tm, HEAD_DIM), lambda i: (i, 0))],
        out_shape=[jax.ShapeDtypeStruct((n, d), F32),
                   jax.ShapeDtypeStruct((n, d), BF16),
                   jax.ShapeDtypeStruct((n, HEAD_DIM), F32)],
        compiler_params=pltpu.CompilerParams(
            dimension_semantics=("parallel",), vmem_limit_bytes=VMEM_LIMIT),
        name="merge_out",
    )(o_a, o_b, proj, proj, x, wb0, wb1, wo, norm_ffn, w_router, b_router)


def _moe_body(be_ref, nu_ref, x_ref, wg_ref, wu_ref, bg_ref, bu_ref, wd_ref, bd_ref, o_ref):
    i = pl.program_id(0)
    f = pl.program_id(1)

    @pl.when(i < nu_ref[0])
    def _():
        x = x_ref[...]
        gate = jnp.dot(x, wg_ref[0], preferred_element_type=F32) + bg_ref[0]
        up = jnp.dot(x, wu_ref[0], preferred_element_type=F32) + bu_ref[0]
        gate = jnp.minimum(gate, SWIGLU_LIMIT)
        up = jnp.clip(up, -SWIGLU_LIMIT, SWIGLU_LIMIT)
        act = (up + 1.0) * gate * _sigmoid(SWIGLU_ALPHA * gate)
        part = jnp.dot(act.astype(BF16), wd_ref[0], preferred_element_type=F32)

        @pl.when(f == 0)
        def _():
            o_ref[...] = part + bd_ref[0]

        @pl.when(f > 0)
        def _():
            o_ref[...] += part


def _moe(block_expert, n_used, xs, w_gu, b_gu, w_d, b_d):
    n_rows, d = xs.shape
    n_exp, _, two_ff = w_gu.shape
    d_ff = two_ff // 2
    tf = min(MOE_FF_TILE, d_ff)
    nf = d_ff // tf
    nblk = n_rows // MOE_ROWS
    grid_spec = pltpu.PrefetchScalarGridSpec(
        num_scalar_prefetch=2,
        grid=(nblk, nf),
        in_specs=[
            pl.BlockSpec((MOE_ROWS, d), lambda i, f, be, nu: (i, 0)),
            pl.BlockSpec((1, d, tf), lambda i, f, be, nu: (be[i], 0, f)),
            pl.BlockSpec((1, d, tf), lambda i, f, be, nu: (be[i], 0, nf + f)),
            pl.BlockSpec((1, 1, tf), lambda i, f, be, nu: (be[i], 0, f)),
            pl.BlockSpec((1, 1, tf), lambda i, f, be, nu: (be[i], 0, nf + f)),
            pl.BlockSpec((1, tf, d), lambda i, f, be, nu: (be[i], f, 0)),
            pl.BlockSpec((1, 1, d), lambda i, f, be, nu: (be[i], 0, 0)),
        ],
        out_specs=pl.BlockSpec((MOE_ROWS, d), lambda i, f, be, nu: (i, 0)),
    )
    return pl.pallas_call(
        _moe_body,
        grid_spec=grid_spec,
        out_shape=jax.ShapeDtypeStruct((n_rows, d), F32),
        compiler_params=pltpu.CompilerParams(
            dimension_semantics=("parallel", "arbitrary"), vmem_limit_bytes=VMEM_LIMIT),
        name="expert_mlp",
    )(block_expert, n_used, xs, w_gu, w_gu, b_gu, b_gu, w_d, b_d)


def _final_body(x_ref, y_ref, g_ref, o_ref):
    x = x_ref[...] + y_ref[...]
    o_ref[...] = x * lax.rsqrt(jnp.mean(x * x, axis=-1, keepdims=True) + EPS) * g_ref[...]


def _final(x1, y, gain, tm):
    n, d = x1.shape
    return pl.pallas_call(
        _final_body,
        grid=(n // tm,),
        in_specs=[pl.BlockSpec((tm, d), lambda i: (i, 0)),
                  pl.BlockSpec((tm, d), lambda i: (i, 0)),
                  pl.BlockSpec((1, d), lambda i: (0, 0))],
        out_specs=pl.BlockSpec((tm, d), lambda i: (i, 0)),
        out_shape=jax.ShapeDtypeStruct((n, d), F32),
        compiler_params=pltpu.CompilerParams(
            dimension_semantics=("parallel",), vmem_limit_bytes=VMEM_LIMIT),
        name="final_norm",
    )(x1, y, gain)


def _token_tile(n, candidates):
    for c in candidates:
        if n % c == 0:
            return c
    raise ValueError(f"token count {n} has no tile in {candidates}")


def _lane_vector(values, first_lane):
    out = jnp.zeros((1, HEAD_DIM), F32)
    return lax.dynamic_update_slice(out, values.reshape(1, -1).astype(F32), (0, first_lane))


def _route(logits, n_exp):
    n_tok = logits.shape[0]
    top_logit, top_idx = lax.top_k(logits, TOP_K)
    gate = jax.nn.softmax(top_logit, axis=-1)
    n_assign = n_tok * TOP_K
    e_flat = top_idx.reshape(-1)
    order = jnp.argsort(e_flat)
    e_sorted = e_flat[order]
    tok_sorted = (order // TOP_K).astype(jnp.int32)
    counts = jnp.bincount(e_flat, length=n_exp)
    padded = ((counts + MOE_ROWS - 1) // MOE_ROWS) * MOE_ROWS
    start = jnp.cumsum(counts) - counts
    pend = jnp.cumsum(padded)
    pstart = pend - padded
    dest = (pstart[e_sorted] + (jnp.arange(n_assign) - start[e_sorted])).astype(jnp.int32)
    n_blocks = -(-n_assign // MOE_ROWS) + n_exp
    row_tok = jnp.zeros((n_blocks * MOE_ROWS,), jnp.int32).at[dest].set(tok_sorted)
    block_expert = jnp.minimum(
        jnp.searchsorted(pend, jnp.arange(n_blocks) * MOE_ROWS, side='right'), n_exp - 1).astype(jnp.int32)
    n_used = (pend[-1] // MOE_ROWS).astype(jnp.int32).reshape(1)
    slot_row = jnp.zeros((n_assign,), jnp.int32).at[order].set(dest).reshape(n_tok, TOP_K)
    return gate, row_tok, block_expert, n_used, slot_row


def kernel(x_prompt, x_sample, state_conv_a, state_delta, state_hgrn, norm_mix, w_in, conv_a, a_log, dt_bias,
           norm_a, lb_logits, norm_b, w_branch, w_out, norm_ffn, w_router, b_router, w_gate_up, b_gate_up,
           w_down, b_down, norm_final):
    bp, tp, d = x_prompt.shape
    bs, ts, _ = x_sample.shape
    depth = w_in.shape[0]
    n_exp = w_router.shape[-1]
    n_p, n_s = bp * tp, bs * ts
    n = n_p + n_s
    assert d == D_HEADS and tp % BLOCK_ROWS == 0 and ts <= BLOCK_ROWS and n_p % ts == 0
    d_conv = 3 * D_HEADS
    tm_big = _token_tile(n, (768, 512, 256))
    tm_small = _token_tile(n, (256,))

    x = jnp.concatenate([x_prompt.reshape(n_p, d), x_sample.reshape(n_s, d)], axis=0)
    conv_p, delta_p, hgrn_p, conv_s, delta_s, hgrn_s = [], [], [], [], [], []
    for l in range(depth):
        w_l = w_in[l]
        c0 = d_conv
        w_main = jnp.concatenate([w_l[:, :c0], w_l[:, c0 + 2 * N_HEADS:]], axis=1).astype(BF16)
        w_ba = jnp.pad(w_l[:, c0:c0 + 2 * N_HEADS], ((0, 0), (0, HEAD_DIM - 2 * N_HEADS))).astype(BF16)
        gain = norm_mix[l].reshape(1, d)
        proj = _inproj(x, gain, w_main, tm_big, 1024)
        ba = _inproj(x, gain, w_ba, tm_big, HEAD_DIM)

        conv_w = jnp.pad(conv_a[l], ((0, CONV_PAD - CONV_WIDTH), (0, 0)))
        alog_l = _lane_vector(a_log[l], N_HEADS)
        dtb_l = _lane_vector(dt_bias[l], N_HEADS)
        na = norm_a[l].reshape(1, HEAD_DIM)
        nb = norm_b[l].reshape(1, HEAD_DIM)
        lbl = jnp.pad(lb_logits.astype(F32), ((0, CONV_PAD - lb_logits.shape[0]), (0, 0)), constant_values=-1e30)

        zero_cs = jnp.zeros((bp, CONV_PAD, d_conv), F32)
        zero_s = jnp.zeros((bp, N_HEADS, HEAD_DIM, HEAD_DIM), F32)
        cs_s = jnp.pad(state_conv_a[l], ((0, 0), (CONV_PAD - (CONV_WIDTH - 1), 0), (0, 0)))

        oa_p, sd_p = _gdn(proj, ba, zero_cs, zero_s, conv_w, alog_l, dtb_l, na,
                          batch=bp, tin=BLOCK_ROWS, nt=tp // BLOCK_ROWS, row0=0)
        oa_s, sd_s = _gdn(proj, ba, cs_s, state_delta[l], conv_w, alog_l, dtb_l, na,
                          batch=bs, tin=ts, nt=1, row0=n_p)
        ob_p, sh_p = _gla(proj, lbl, nb, zero_s, batch=bp, tin=BLOCK_ROWS, nt=tp // BLOCK_ROWS, row0=0, layer=l)
        ob_s, sh_s = _gla(proj, lbl, nb, jnp.swapaxes(state_hgrn[l], -1, -2),
                          batch=bs, tin=ts, nt=1, row0=n_p, layer=l)
        o_a = jnp.concatenate([oa_p, oa_s], axis=0)
        o_b = jnp.concatenate([ob_p, ob_s], axis=0)

        qkv_p = proj[:n_p, :d_conv].reshape(bp, tp, d_conv)
        qkv_s = proj[n_p:, :d_conv].reshape(bs, ts, d_conv)
        conv_p.append(jnp.concatenate([jnp.zeros((bp, CONV_WIDTH - 1, d_conv), F32), qkv_p], axis=1)[:, -(CONV_WIDTH - 1):])
        conv_s.append(jnp.concatenate([state_conv_a[l], qkv_s], axis=1)[:, -(CONV_WIDTH - 1):])
        delta_p.append(sd_p)
        delta_s.append(sd_s)
        hgrn_p.append(jnp.swapaxes(sh_p, -1, -2))
        hgrn_s.append(jnp.swapaxes(sh_s, -1, -2))

        w_r = jnp.pad(w_router[l], ((0, 0), (0, HEAD_DIM - n_exp)))
        b_r = jnp.pad(b_router[l], (0, HEAD_DIM - n_exp)).reshape(1, HEAD_DIM)
        x1, h, logits = _merge(o_a, o_b, proj, x, w_branch[l, 0].astype(BF16), w_branch[l, 1].astype(BF16),
                               w_out[l].astype(BF16), norm_ffn[l].reshape(1, d), w_r, b_r, tm_small)

        gate, row_tok, block_expert, n_used, slot_row = _route(logits[:, :n_exp], n_exp)
        xs = jnp.take(h, row_tok, axis=0)
        ys = _moe(block_expert, n_used, xs, w_gate_up[l].astype(BF16), b_gate_up[l][:, None, :],
                  w_down[l].astype(BF16), b_down[l][:, None, :])
        y = jnp.einsum('tk,tkd->td', gate, jnp.take(ys, slot_row, axis=0))
        if l + 1 < depth:
            x = x1 + y
    out = _final(x1, y, norm_final.reshape(1, d), tm_big)
    y_prompt = out[:n_p].reshape(bp, tp, d)
    y_sample = out[n_p:].reshape(bs, ts, d)
    return (y_prompt, y_sample, jnp.stack(conv_p), jnp.stack(delta_p), jnp.stack(hgrn_p),
            jnp.stack(conv_s), jnp.stack(delta_s), jnp.stack(hgrn_s))
```

```python
import functools

import jax
import jax.numpy as jnp
from jax import lax
from jax.experimental import pallas as pl
from jax.experimental.pallas import tpu as pltpu

F32 = jnp.float32
BF16 = jnp.bfloat16
HIGHEST = lax.Precision.HIGHEST

EPS = 1e-6
HEAD_DIM = 128
N_HEADS = 16
D_HEADS = N_HEADS * HEAD_DIM
CONV_WIDTH = 4
CONV_PAD = 8
BLOCK_ROWS = 256
GLA_CHUNK = 64
GLA_SUB = 8
HEAD_GROUP = 2
GDN_HEAD_GROUP = 4
DIAG_UNROLL = True
TOP_K = 4
SWIGLU_LIMIT = 7.0
SWIGLU_ALPHA = 1.702
MOE_ROWS = 512
MOE_FF_TILE = 512
VMEM_LIMIT = 56 * 1024 * 1024

NT_DIMS = (((1,), (1,)), ((), ()))
TN_DIMS = (((0,), (0,)), ((), ()))


def _sigmoid(x):
    return 1.0 / (1.0 + jnp.exp(-x))


def _bdot(a, b):
    return jnp.dot(a.astype(BF16), b.astype(BF16), preferred_element_type=F32)


def _bdot_general(a, b, dims):
    return lax.dot_general(a.astype(BF16), b.astype(BF16), dims, preferred_element_type=F32)


def _hdot(a, b):
    return jnp.dot(a, b, precision=HIGHEST, preferred_element_type=F32)


def _trace_round_robin(gens):
    results = [None] * len(gens)
    live = list(range(len(gens)))
    while live:
        for i in list(live):
            try:
                next(gens[i])
            except StopIteration as stop:
                results[i] = stop.value
                live.remove(i)
    return results


def _inproj_body(x_ref, g_ref, w_ref, o_ref, xn_ref):
    @pl.when(pl.program_id(1) == 0)
    def _():
        x = x_ref[...]
        ms = jnp.mean(x * x, axis=-1, keepdims=True)
        xn_ref[...] = (x * lax.rsqrt(ms + EPS) * g_ref[...]).astype(BF16)

    o_ref[...] = jnp.dot(xn_ref[...], w_ref[...], preferred_element_type=F32)


def _inproj(x, gain, w, tm, tn):
    n, d = x.shape
    nc = w.shape[1]
    return pl.pallas_call(
        _inproj_body,
        grid=(n // tm, nc // tn),
        in_specs=[pl.BlockSpec((tm, d), lambda i, j: (i, 0)),
                  pl.BlockSpec((1, d), lambda i, j: (0, 0)),
                  pl.BlockSpec((d, tn), lambda i, j: (0, j))],
        out_specs=pl.BlockSpec((tm, tn), lambda i, j: (i, j)),
        out_shape=jax.ShapeDtypeStruct((n, nc), F32),
        scratch_shapes=[pltpu.VMEM((tm, d), BF16)],
        compiler_params=pltpu.CompilerParams(
            dimension_semantics=("parallel", "arbitrary"), vmem_limit_bytes=VMEM_LIMIT),
        name="inproj",
    )(x, gain, w)


def _pad_rows(x, rows):
    if x.shape[0] == rows:
        return x
    return jnp.concatenate([x, jnp.zeros((rows - x.shape[0],) + x.shape[1:], x.dtype)], axis=0)


def _gdn_body(qkv_ref, z_ref, ba_ref, cs_ref, s0_ref, cw_ref, alog_ref, dtb_ref, na_ref,
              o_ref, sout_ref, xbuf, s_scr, *, tin, nt):
    R = BLOCK_ROWS
    t = pl.program_id(1)

    @pl.when(t == 0)
    def _():
        xbuf[0:CONV_PAD, :] = cs_ref[0]
        s_scr[...] = s0_ref[0]

    if nt > 1:
        @pl.when(t > 0)
        def _():
            xbuf[0:CONV_PAD, :] = xbuf[tin:tin + CONV_PAD, :]

    xbuf[CONV_PAD:CONV_PAD + tin, :] = qkv_ref[...]
    if tin < R:
        xbuf[CONV_PAD + tin:CONV_PAD + R, :] = jnp.zeros((R - tin, xbuf.shape[1]), F32)

    rows = lax.broadcasted_iota(jnp.int32, (R, HEAD_DIM), 0)
    lanes = lax.broadcasted_iota(jnp.int32, (R, HEAD_DIM), 1)
    valid = rows < tin
    ti = lax.broadcasted_iota(jnp.int32, (R, R), 0)
    si = lax.broadcasted_iota(jnp.int32, (R, R), 1)
    causal = ti >= si
    merge_key = jnp.where(ti > si, ti ^ si, 0)

    ba = _pad_rows(ba_ref[...], R)
    sp_in = ba + dtb_ref[...]
    softplus = jnp.maximum(sp_in, 0.0) + jnp.log(1.0 + jnp.exp(-jnp.abs(sp_in)))
    g_all = jnp.where(valid, -jnp.exp(alog_ref[...]) * softplus, 0.0)
    beta_all = jnp.where(valid, _sigmoid(ba), 0.0)
    gcum_all = _hdot(causal.astype(F32), g_all)
    gate_src = jnp.where(lanes < N_HEADS, beta_all, gcum_all)

    sel_l = lax.broadcasted_iota(jnp.int32, (HEAD_DIM, 2 * HEAD_DIM), 0)
    sel_c = lax.broadcasted_iota(jnp.int32, (HEAD_DIM, 2 * HEAD_DIM), 1)
    lane0 = (lanes == 0).astype(F32)

    def head(h, s_old):
        off = h * HEAD_DIM

        def conv_silu(base):
            cols = pl.ds(pl.multiple_of(base + off, HEAD_DIM), HEAD_DIM)
            acc = None
            for j in range(CONV_WIDTH):
                term = xbuf[pl.ds(CONV_PAD - (CONV_WIDTH - 1) + j, R), cols] * cw_ref[j:j + 1, cols]
                acc = term if acc is None else acc + term
            return acc * _sigmoid(acc)

        qc = conv_silu(0)
        kc = conv_silu(D_HEADS)
        vc = conv_silu(2 * D_HEADS)
        qn = qc * lax.rsqrt(jnp.sum(qc * qc, axis=-1, keepdims=True) + EPS) * (HEAD_DIM ** -0.5)
        kn = jnp.where(valid, kc * lax.rsqrt(jnp.sum(kc * kc, axis=-1, keepdims=True) + EPS), 0.0)

        sel = (((sel_c < HEAD_DIM) & (sel_l == h)) |
               ((sel_c >= HEAD_DIM) & (sel_l == h + N_HEADS))).astype(F32)
        bg = _hdot(gate_src, sel)
        yield
        beta = bg[:, :HEAD_DIM]
        gc = bg[:, HEAD_DIM:]
        gc_last = gc[R - 1:R, :]
        gc_t = jnp.concatenate([gc, gc], axis=1)
        gc_s = lax.dot_general(lane0, gc, NT_DIMS, precision=HIGHEST, preferred_element_type=F32)
        yield
        decay = jnp.where(causal, jnp.exp(jnp.where(causal, gc_t - gc_s, 0.0)), 0.0)

        kb = kn * beta
        aq = _bdot_general(jnp.concatenate([kb, qn], axis=0), kn, NT_DIMS)
        yield
        a_mat = jnp.where(ti > si, aq[:R] * decay, 0.0)
        qk = aq[R:] * decay

        n_mat = None
        b = 1
        while b < R:
            c_m = jnp.where((merge_key >= b) & (merge_key < 2 * b), a_mat, 0.0)
            if n_mat is None:
                n_mat = -c_m
            else:
                p = c_m + _bdot(n_mat, c_m)
                yield
                n_mat = n_mat - p - _bdot(p, n_mat)
                yield
            b *= 2

        rhs = jnp.concatenate([vc * beta, kb * jnp.exp(gc)], axis=1)
        sol = rhs + _bdot(n_mat, rhs)
        yield
        u = sol[:, :HEAD_DIM]
        w = sol[:, HEAD_DIM:]

        qd = qn * jnp.exp(gc)
        kd = kn * jnp.exp(gc_last - gc)
        ws = _bdot(jnp.concatenate([w, qd], axis=0), s_old)
        yield
        v_new = u - ws[:R]
        o = ws[R:] + _bdot(qk, v_new)
        s_new = s_old * jnp.exp(gc_last) + _bdot_general(kd, v_new, TN_DIMS)
        yield

        cols = pl.ds(pl.multiple_of(off, HEAD_DIM), HEAD_DIM)
        on = o * lax.rsqrt(jnp.mean(o * o, axis=-1, keepdims=True) + EPS) * na_ref[...]
        z = z_ref[:, cols]
        o_ref[:, cols] = (on[:tin] * (z * _sigmoid(z))).astype(BF16)
        return s_new

    def head_group(hg, carry):
        heads = [hg * GDN_HEAD_GROUP + u for u in range(GDN_HEAD_GROUP)]
        states = _trace_round_robin([head(h, s_scr[h]) for h in heads])
        for h, s in zip(heads, states):
            s_scr[h] = s
        return carry

    lax.fori_loop(0, N_HEADS // GDN_HEAD_GROUP, head_group, 0)

    @pl.when(t == nt - 1)
    def _():
        sout_ref[0] = s_scr[...]


def _gdn(proj, ba, conv_state, s0, conv_w, alog_l, dtb_l, norm_a, *, batch, tin, nt, row0):
    rb0 = row0 // tin
    body = functools.partial(_gdn_body, tin=tin, nt=nt)
    return pl.pallas_call(
        body,
        grid=(batch, nt),
        in_specs=[
            pl.BlockSpec((tin, 3 * D_HEADS), lambda b, t: (rb0 + b * nt + t, 0)),
            pl.BlockSpec((tin, D_HEADS), lambda b, t: (rb0 + b * nt + t, 3)),
            pl.BlockSpec((tin, HEAD_DIM), lambda b, t: (rb0 + b * nt + t, 0)),
            pl.BlockSpec((1, CONV_PAD, 3 * D_HEADS), lambda b, t: (b, 0, 0)),
            pl.BlockSpec((1, N_HEADS, HEAD_DIM, HEAD_DIM), lambda b, t: (b, 0, 0, 0)),
            pl.BlockSpec((CONV_PAD, 3 * D_HEADS), lambda b, t: (0, 0)),
            pl.BlockSpec((1, HEAD_DIM), lambda b, t: (0, 0)),
            pl.BlockSpec((1, HEAD_DIM), lambda b, t: (0, 0)),
            pl.BlockSpec((1, HEAD_DIM), lambda b, t: (0, 0)),
        ],
        out_specs=[
            pl.BlockSpec((tin, D_HEADS), lambda b, t: (b * nt + t, 0)),
            pl.BlockSpec((1, N_HEADS, HEAD_DIM, HEAD_DIM), lambda b, t: (b, 0, 0, 0)),
        ],
        out_shape=[jax.ShapeDtypeStruct((batch * nt * tin, D_HEADS), BF16),
                   jax.ShapeDtypeStruct((batch, N_HEADS, HEAD_DIM, HEAD_DIM), F32)],
        scratch_shapes=[pltpu.VMEM((BLOCK_ROWS + CONV_PAD, 3 * D_HEADS), F32),
                        pltpu.VMEM((N_HEADS, HEAD_DIM, HEAD_DIM), F32)],
        compiler_params=pltpu.CompilerParams(
            dimension_semantics=("parallel", "arbitrary"), vmem_limit_bytes=VMEM_LIMIT),
        name="gated_delta",
    )(proj, proj, ba, conv_state, s0, conv_w, alog_l, dtb_l, norm_a)


def _gla_body(q_ref, f_ref, i_ref, z_ref, lbl_ref, nb_ref, s0_ref, o_ref, sout_ref,
              s_scr, *row_scr, tin, nt, layer):
    R = BLOCK_ROWS
    n_chunks = R // GLA_CHUNK
    t = pl.program_id(1)

    @pl.when(t == 0)
    def _():
        s_scr[...] = s0_ref[0]

    rows = lax.broadcasted_iota(jnp.int32, (R, HEAD_DIM), 0)
    valid = rows < tin
    ti = lax.broadcasted_iota(jnp.int32, (R, R), 0)
    si = lax.broadcasted_iota(jnp.int32, (R, R), 1)
    chunk_tri = ((ti >= si) & ((ti // GLA_CHUNK) == (si // GLA_CHUNK))).astype(F32)
    pair_key = jnp.where(ti > si, ti ^ si, 0)
    row_local = lax.broadcasted_iota(jnp.int32, (GLA_SUB, HEAD_DIM), 0)

    def head_pre(h, slot):
        q_scr, k_scr, v_scr, b_scr, _ = (r.at[slot] for r in row_scr)
        cols = pl.ds(pl.multiple_of(h * HEAD_DIM, HEAD_DIM), HEAD_DIM)
        lbl = lbl_ref[:, cols]
        lb_e = jnp.exp(lbl - jnp.max(lbl, axis=0, keepdims=True))
        lb = jnp.sum(lb_e[0:layer + 1], axis=0, keepdims=True) / jnp.sum(lb_e, axis=0, keepdims=True)

        fl = _pad_rows(f_ref[:, cols], R)
        qx = _pad_rows(q_ref[:, cols], R)
        vv = _pad_rows(i_ref[:, cols], R)
        log_f = jnp.where(valid, jnp.log(lb + (1.0 - lb) * _sigmoid(fl)), 0.0)
        kk = jnp.where(valid, (1.0 - lb) * _sigmoid(-fl), 0.0)
        qq = qx * _sigmoid(qx)
        bc = _hdot(chunk_tri, log_f)
        yield
        q_scr[...] = qq
        k_scr[...] = kk
        v_scr[...] = vv
        b_scr[...] = bc

        def block_row(period, row):
            return jnp.concatenate(
                [jnp.broadcast_to(b_scr[start + row:start + row + 1, :], (period, HEAD_DIM))
                 for start in range(0, R, period)], axis=0)

        b_last = block_row(GLA_CHUNK, GLA_CHUNK - 1)
        qd = qq * jnp.exp(bc)
        kd = kk * jnp.exp(b_last - bc)

        scores = jnp.zeros((R, R), F32)
        hs = GLA_SUB
        while hs < GLA_CHUNK:
            b_mid = block_row(2 * hs, hs - 1)
            lower = (rows & hs) != 0
            q_fac = jnp.where(lower, qq * jnp.exp(jnp.minimum(bc - b_mid, 0.0)), 0.0)
            k_fac = jnp.where(lower, 0.0, kk * jnp.exp(jnp.minimum(b_mid - bc, 0.0)))
            level = _bdot_general(q_fac, k_fac, NT_DIMS)
            yield
            scores = jnp.where((pair_key >= hs) & (pair_key < 2 * hs), level, scores)
            hs *= 2
        o_off = _bdot(scores, vv)
        yield
        return qd, kd, vv, o_off

    def diag_block(m, carry):
        r0 = pl.multiple_of(m * GLA_SUB, GLA_SUB)
        for slot in range(HEAD_GROUP):
            q_scr, k_scr, v_scr, b_scr, od_scr = (r.at[slot] for r in row_scr)
            q_blk = q_scr[pl.ds(r0, GLA_SUB), :]
            b_blk = b_scr[pl.ds(r0, GLA_SUB), :]
            acc = jnp.zeros((GLA_SUB, HEAD_DIM), F32)
            for s in range(GLA_SUB):
                b_s = b_scr[pl.ds(r0 + s, 1), :]
                k_s = k_scr[pl.ds(r0 + s, 1), :]
                v_s = v_scr[pl.ds(r0 + s, 1), :]
                e = jnp.where(row_local >= s, jnp.exp(jnp.minimum(b_blk - b_s, 0.0)), 0.0)
                col = jnp.sum(q_blk * k_s * e, axis=-1, keepdims=True)
                acc = acc + col * v_s
            od_scr[pl.ds(r0, GLA_SUB), :] = acc
        return carry

    def head_post(h, slot, pre, s_t):
        b_scr, od_scr = row_scr[3].at[slot], row_scr[4].at[slot]
        qd, kd, vv, o_off = pre
        o_inter = []
        for c in range(n_chunks):
            sl = slice(c * GLA_CHUNK, (c + 1) * GLA_CHUNK)
            o_inter.append(_bdot_general(qd[sl], s_t, NT_DIMS))
            f_last = jnp.exp(b_scr[(c + 1) * GLA_CHUNK - 1:(c + 1) * GLA_CHUNK, :])
            s_t = s_t * f_last + _bdot_general(vv[sl], kd[sl], TN_DIMS)
            yield

        o = jnp.concatenate(o_inter, axis=0) + o_off + od_scr[...]
        on = o * lax.rsqrt(jnp.mean(o * o, axis=-1, keepdims=True) + EPS) * nb_ref[...]
        cols = pl.ds(pl.multiple_of(h * HEAD_DIM, HEAD_DIM), HEAD_DIM)
        z = z_ref[:, cols]
        o_ref[:, cols] = (on[:tin] * _sigmoid(z)).astype(BF16)
        return s_t

    def head_group(hg, carry):
        heads = [hg * HEAD_GROUP + u for u in range(HEAD_GROUP)]
        states = [s_scr[h] for h in heads]
        pre = _trace_round_robin([head_pre(h, u) for u, h in enumerate(heads)])
        lax.fori_loop(0, R // GLA_SUB, diag_block, 0, unroll=DIAG_UNROLL)
        states = _trace_round_robin([head_post(h, u, p, s) for u, (h, p, s) in enumerate(zip(heads, pre, states))])
        for h, s in zip(heads, states):
            s_scr[h] = s
        return carry

    lax.fori_loop(0, N_HEADS // HEAD_GROUP, head_group, 0)

    @pl.when(t == nt - 1)
    def _():
        sout_ref[0] = s_scr[...]


def _gla(proj, lb_logits, norm_b, s0_t, *, batch, tin, nt, row0, layer):
    rb0 = row0 // tin
    body = functools.partial(_gla_body, tin=tin, nt=nt, layer=layer)
    col_spec = lambda c: pl.BlockSpec((tin, D_HEADS), lambda b, t: (rb0 + b * nt + t, c))
    head_rows = pltpu.VMEM((HEAD_GROUP, BLOCK_ROWS, HEAD_DIM), F32)
    return pl.pallas_call(
        body,
        grid=(batch, nt),
        in_specs=[
            col_spec(4), col_spec(5), col_spec(6), col_spec(7),
            pl.BlockSpec(lb_logits.shape, lambda b, t: (0, 0)),
            pl.BlockSpec((1, HEAD_DIM), lambda b, t: (0, 0)),
            pl.BlockSpec((1, N_HEADS, HEAD_DIM, HEAD_DIM), lambda b, t: (b, 0, 0, 0)),
        ],
        out_specs=[
            pl.BlockSpec((tin, D_HEADS), lambda b, t: (b * nt + t, 0)),
            pl.BlockSpec((1, N_HEADS, HEAD_DIM, HEAD_DIM), lambda b, t: (b, 0, 0, 0)),
        ],
        out_shape=[jax.ShapeDtypeStruct((batch * nt * tin, D_HEADS), BF16),
                   jax.ShapeDtypeStruct((batch, N_HEADS, HEAD_DIM, HEAD_DIM), F32)],
        scratch_shapes=[pltpu.VMEM((N_HEADS, HEAD_DIM, HEAD_DIM), F32),
                        head_rows, head_rows, head_rows, head_rows, head_rows],
        compiler_params=pltpu.CompilerParams(
            dimension_semantics=("parallel", "arbitrary"), vmem_limit_bytes=VMEM_LIMIT),
        name="hgrn2",
    )(proj, proj, proj, proj, lb_logits, norm_b, s0_t)


def _merge_body(oa_ref, ob_ref, g0_ref, g1_ref, x_ref, wb0_ref, wb1_ref, wo_ref, nf_ref, wr_ref, br_ref,
                x1_ref, h_ref, lg_ref):
    ua = jnp.dot(oa_ref[...], wb0_ref[...], preferred_element_type=F32)
    ub = jnp.dot(ob_ref[...], wb1_ref[...], preferred_element_type=F32)
    merged = _sigmoid(g0_ref[...]) * ua + _sigmoid(g1_ref[...]) * ub
    x1 = x_ref[...] + jnp.dot(merged.astype(BF16), wo_ref[...], preferred_element_type=F32)
    x1_ref[...] = x1
    hn = x1 * lax.rsqrt(jnp.mean(x1 * x1, axis=-1, keepdims=True) + EPS) * nf_ref[...]
    h_ref[...] = hn.astype(BF16)
    lg_ref[...] = _hdot(hn, wr_ref[...]) + br_ref[...]


def _merge(o_a, o_b, proj, x, wb0, wb1, wo, norm_ffn, w_router, b_router, tm):
    n, d = x.shape
    const = lambda shape: pl.BlockSpec(shape, lambda i: (0, 0), pipeline_mode=pl.Buffered(1))
    return pl.pallas_call(
        _merge_body,
        grid=(n // tm,),
        in_specs=[
            pl.BlockSpec((tm, D_HEADS), lambda i: (i, 0)),
            pl.BlockSpec((tm, D_HEADS), lambda i: (i, 0)),
            pl.BlockSpec((tm, d), lambda i: (i, 8)),
            pl.BlockSpec((tm, d), lambda i: (i, 9)),
            pl.BlockSpec((tm, d), lambda i: (i, 0)),
            const(wb0.shape), const(wb1.shape), const(wo.shape),
            const((1, d)), const(w_router.shape), const((1, HEAD_DIM)),
        ],
        out_specs=[pl.BlockSpec((tm, d), lambda i: (i, 0)),
                   pl.BlockSpec((tm, d), lambda i: (i, 0)),
                   pl.BlockSpec((tm, HEAD_DIM), lambda i: (i, 0))],
        out_shape=[jax.ShapeDtypeStruct((n, d), F32),
                   jax.ShapeDtypeStruct((n, d), BF16),
                   jax.ShapeDtypeStruct((n, HEAD_DIM), F32)],
        compiler_params=pltpu.CompilerParams(
            dimension_semantics=("parallel",), vmem_limit_bytes=VMEM_LIMIT),
        name="merge_out",
    )(o_a, o_b, proj, proj, x, wb0, wb1, wo, norm_ffn, w_router, b_router)


def _moe_body(be_ref, nu_ref, x_ref, wg_ref, wu_ref, bg_ref, bu_ref, wd_ref, bd_ref, o_ref):
    i = pl.program_id(0)
    f = pl.program_id(1)

    @pl.when(i < nu_ref[0])
    def _():
        x = x_ref[...]
        gate = jnp.dot(x, wg_ref[0], preferred_element_type=F32) + bg_ref[0]
        up = jnp.dot(x, wu_ref[0], preferred_element_type=F32) + bu_ref[0]
        gate = jnp.minimum(gate, SWIGLU_LIMIT)
        up = jnp.clip(up, -SWIGLU_LIMIT, SWIGLU_LIMIT)
        act = (up + 1.0) * gate * _sigmoid(SWIGLU_ALPHA * gate)
        part = jnp.dot(act.astype(BF16), wd_ref[0], preferred_element_type=F32)

        @pl.when(f == 0)
        def _():
            o_ref[...] = part + bd_ref[0]

        @pl.when(f > 0)
        def _():
            o_ref[...] += part


def _moe(block_expert, n_used, xs, w_gu, b_gu, w_d, b_d):
    n_rows, d = xs.shape
    n_exp, _, two_ff = w_gu.shape
    d_ff = two_ff // 2
    tf = min(MOE_FF_TILE, d_ff)
    nf = d_ff // tf
    nblk = n_rows // MOE_ROWS
    grid_spec = pltpu.PrefetchScalarGridSpec(
        num_scalar_prefetch=2,
        grid=(nblk, nf),
        in_specs=[
            pl.BlockSpec((MOE_ROWS, d), lambda i, f, be, nu: (i, 0)),
            pl.BlockSpec((1, d, tf), lambda i, f, be, nu: (be[i], 0, f)),
            pl.BlockSpec((1, d, tf), lambda i, f, be, nu: (be[i], 0, nf + f)),
            pl.BlockSpec((1, 1, tf), lambda i, f, be, nu: (be[i], 0, f)),
            pl.BlockSpec((1, 1, tf), lambda i, f, be, nu: (be[i], 0, nf + f)),
            pl.BlockSpec((1, tf, d), lambda i, f, be, nu: (be[i], f, 0)),
            pl.BlockSpec((1, 1, d), lambda i, f, be, nu: (be[i], 0, 0)),
        ],
        out_specs=pl.BlockSpec((MOE_ROWS, d), lambda i, f, be, nu: (i, 0)),
    )
    return pl.pallas_call(
        _moe_body,
        grid_spec=grid_spec,
        out_shape=jax.ShapeDtypeStruct((n_rows, d), F32),
        compiler_params=pltpu.CompilerParams(
            dimension_semantics=("parallel", "arbitrary"), vmem_limit_bytes=VMEM_LIMIT),
        name="expert_mlp",
    )(block_expert, n_used, xs, w_gu, w_gu, b_gu, b_gu, w_d, b_d)


def _final_body(x_ref, y_ref, g_ref, o_ref):
    x = x_ref[...] + y_ref[...]
    o_ref[...] = x * lax.rsqrt(jnp.mean(x * x, axis=-1, keepdims=True) + EPS) * g_ref[...]


def _final(x1, y, gain, tm):
    n, d = x1.shape
    return pl.pallas_call(
        _final_body,
        grid=(n // tm,),
        in_specs=[pl.BlockSpec((tm, d), lambda i: (i, 0)),
                  pl.BlockSpec((tm, d), lambda i: (i, 0)),
                  pl.BlockSpec((1, d), lambda i: (0, 0))],
        out_specs=pl.BlockSpec((tm, d), lambda i: (i, 0)),
        out_shape=jax.ShapeDtypeStruct((n, d), F32),
        compiler_params=pltpu.CompilerParams(
            dimension_semantics=("parallel",), vmem_limit_bytes=VMEM_LIMIT),
        name="final_norm",
    )(x1, y, gain)


def _token_tile(n, candidates):
    for c in candidates:
        if n % c == 0:
            return c
    raise ValueError(f"token count {n} has no tile in {candidates}")


def _lane_vector(values, first_lane):
    out = jnp.zeros((1, HEAD_DIM), F32)
    return lax.dynamic_update_slice(out, values.reshape(1, -1).astype(F32), (0, first_lane))


def _route(logits, n_exp):
    n_tok = logits.shape[0]
    top_logit, top_idx = lax.top_k(logits, TOP_K)
    gate = jax.nn.softmax(top_logit, axis=-1)
    n_assign = n_tok * TOP_K
    e_flat = top_idx.reshape(-1)
    order = jnp.argsort(e_flat)
    e_sorted = e_flat[order]
    tok_sorted = (order // TOP_K).astype(jnp.int32)
    counts = jnp.bincount(e_flat, length=n_exp)
    padded = ((counts + MOE_ROWS - 1) // MOE_ROWS) * MOE_ROWS
    start = jnp.cumsum(counts) - counts
    pend = jnp.cumsum(padded)
    pstart = pend - padded
    dest = (pstart[e_sorted] + (jnp.arange(n_assign) - start[e_sorted])).astype(jnp.int32)
    n_blocks = -(-n_assign // MOE_ROWS) + n_exp
    row_tok = jnp.zeros((n_blocks * MOE_ROWS,), jnp.int32).at[dest].set(tok_sorted)
    block_expert = jnp.minimum(
        jnp.searchsorted(pend, jnp.arange(n_blocks) * MOE_ROWS, side='right'), n_exp - 1).astype(jnp.int32)
    n_used = (pend[-1] // MOE_ROWS).astype(jnp.int32).reshape(1)
    slot_row = jnp.zeros((n_assign,), jnp.int32).at[order].set(dest).reshape(n_tok, TOP_K)
    return gate, row_tok, block_expert, n_used, slot_row


def kernel(x_prompt, x_sample, state_conv_a, state_delta, state_hgrn, norm_mix, w_in, conv_a, a_log, dt_bias,
           norm_a, lb_logits, norm_b, w_branch, w_out, norm_ffn, w_router, b_router, w_gate_up, b_gate_up,
           w_down, b_down, norm_final):
    bp, tp, d = x_prompt.shape
    bs, ts, _ = x_sample.shape
    depth = w_in.shape[0]
    n_exp = w_router.shape[-1]
    n_p, n_s = bp * tp, bs * ts
    n = n_p + n_s
    assert d == D_HEADS and tp % BLOCK_ROWS == 0 and ts <= BLOCK_ROWS and n_p % ts == 0
    assert min(tp, ts) >= CONV_WIDTH - 1
    d_conv = 3 * D_HEADS
    tm_big = _token_tile(n, (768, 512, 256))
    tm_small = _token_tile(n, (256,))

    x = jnp.concatenate([x_prompt.reshape(n_p, d), x_sample.reshape(n_s, d)], axis=0)
    conv_p, delta_p, hgrn_p, conv_s, delta_s, hgrn_s = [], [], [], [], [], []
    for l in range(depth):
        w_l = w_in[l]
        c0 = d_conv
        w_main = jnp.concatenate([w_l[:, :c0], w_l[:, c0 + 2 * N_HEADS:]], axis=1).astype(BF16)
        w_ba = jnp.pad(w_l[:, c0:c0 + 2 * N_HEADS], ((0, 0), (0, HEAD_DIM - 2 * N_HEADS))).astype(BF16)
        gain = norm_mix[l].reshape(1, d)
        proj = _inproj(x, gain, w_main, tm_big, 1024)
        ba = _inproj(x, gain, w_ba, tm_big, HEAD_DIM)

        conv_w = jnp.pad(conv_a[l], ((0, CONV_PAD - CONV_WIDTH), (0, 0)))
        alog_l = _lane_vector(a_log[l], N_HEADS)
        dtb_l = _lane_vector(dt_bias[l], N_HEADS)
        na = norm_a[l].reshape(1, HEAD_DIM)
        nb = norm_b[l].reshape(1, HEAD_DIM)
        lbl = jnp.pad(lb_logits.astype(F32), ((0, CONV_PAD - lb_logits.shape[0]), (0, 0)), constant_values=-1e30)

        zero_cs = jnp.zeros((bp, CONV_PAD, d_conv), F32)
        zero_s = jnp.zeros((bp, N_HEADS, HEAD_DIM, HEAD_DIM), F32)
        cs_s = jnp.pad(state_conv_a[l], ((0, 0), (CONV_PAD - (CONV_WIDTH - 1), 0), (0, 0)))

        oa_p, sd_p = _gdn(proj, ba, zero_cs, zero_s, conv_w, alog_l, dtb_l, na,
                          batch=bp, tin=BLOCK_ROWS, nt=tp // BLOCK_ROWS, row0=0)
        oa_s, sd_s = _gdn(proj, ba, cs_s, state_delta[l], conv_w, alog_l, dtb_l, na,
                          batch=bs, tin=ts, nt=1, row0=n_p)
        ob_p, sh_p = _gla(proj, lbl, nb, zero_s, batch=bp, tin=BLOCK_ROWS, nt=tp // BLOCK_ROWS, row0=0, layer=l)
        ob_s, sh_s = _gla(proj, lbl, nb, jnp.swapaxes(state_hgrn[l], -1, -2),
                          batch=bs, tin=ts, nt=1, row0=n_p, layer=l)
        o_a = jnp.concatenate([oa_p, oa_s], axis=0)
        o_b = jnp.concatenate([ob_p, ob_s], axis=0)

        tail = CONV_WIDTH - 1
        conv_p.append(jnp.stack([proj[(b + 1) * tp - tail:(b + 1) * tp, :d_conv] for b in range(bp)]))
        conv_s.append(jnp.stack([proj[n_p + (b + 1) * ts - tail:n_p + (b + 1) * ts, :d_conv] for b in range(bs)]))
        delta_p.append(sd_p)
        delta_s.append(sd_s)
        hgrn_p.append(jnp.swapaxes(sh_p, -1, -2))
        hgrn_s.append(jnp.swapaxes(sh_s, -1, -2))

        w_r = jnp.pad(w_router[l], ((0, 0), (0, HEAD_DIM - n_exp)))
        b_r = jnp.pad(b_router[l], (0, HEAD_DIM - n_exp)).reshape(1, HEAD_DIM)
        x1, h, logits = _merge(o_a, o_b, proj, x, w_branch[l, 0].astype(BF16), w_branch[l, 1].astype(BF16),
                               w_out[l].astype(BF16), norm_ffn[l].reshape(1, d), w_r, b_r, tm_small)

        gate, row_tok, block_expert, n_used, slot_row = _route(logits[:, :n_exp], n_exp)
        xs = jnp.take(h, row_tok, axis=0)
        ys = _moe(block_expert, n_used, xs, w_gate_up[l].astype(BF16), b_gate_up[l][:, None, :],
                  w_down[l].astype(BF16), b_down[l][:, None, :])
        y = jnp.einsum('tk,tkd->td', gate, jnp.take(ys, slot_row, axis=0))
        if l + 1 < depth:
            x = x1 + y
    out = _final(x1, y, norm_final.reshape(1, d), tm_big)
    y_prompt = out[:n_p].reshape(bp, tp, d)
    y_sample = out[n_p:].reshape(bs, ts, d)
    return (y_prompt, y_sample, jnp.stack(conv_p), jnp.stack(delta_p), jnp.stack(hgrn_p),
            jnp.stack(conv_s), jnp.stack(delta_s), jnp.stack(hgrn_s))
```

```python
import functools

import jax
import jax.numpy as jnp
from jax import lax
from jax.experimental import pallas as pl
from jax.experimental.pallas import tpu as pltpu

F32 = jnp.float32
BF16 = jnp.bfloat16
HIGHEST = lax.Precision.HIGHEST

EPS = 1e-6
HEAD_DIM = 128
N_HEADS = 16
D_HEADS = N_HEADS * HEAD_DIM
CONV_WIDTH = 4
CONV_PAD = 8
BLOCK_ROWS = 256
GLA_CHUNK = 64
GLA_SUB = 8
HEAD_GROUP = 2
GDN_HEAD_GROUP = 4
DIAG_UNROLL = True
TOP_K = 4
SWIGLU_LIMIT = 7.0
SWIGLU_ALPHA = 1.702
MOE_ROWS = 512
MOE_FF_TILE = 512
ROW_COPY_UNROLL = 2
VMEM_LIMIT = 56 * 1024 * 1024

NT_DIMS = (((1,), (1,)), ((), ()))
TN_DIMS = (((0,), (0,)), ((), ()))


def _sigmoid(x):
    return 1.0 / (1.0 + jnp.exp(-x))


def _bdot(a, b):
    return jnp.dot(a.astype(BF16), b.astype(BF16), preferred_element_type=F32)


def _bdot_general(a, b, dims):
    return lax.dot_general(a.astype(BF16), b.astype(BF16), dims, preferred_element_type=F32)


def _hdot(a, b):
    return jnp.dot(a, b, precision=HIGHEST, preferred_element_type=F32)


def _trace_round_robin(gens):
    results = [None] * len(gens)
    live = list(range(len(gens)))
    while live:
        for i in list(live):
            try:
                next(gens[i])
            except StopIteration as stop:
                results[i] = stop.value
                live.remove(i)
    return results


def _inproj_body(x_ref, g_ref, w_ref, o_ref, xn_ref):
    @pl.when(pl.program_id(1) == 0)
    def _():
        x = x_ref[...]
        ms = jnp.mean(x * x, axis=-1, keepdims=True)
        xn_ref[...] = (x * lax.rsqrt(ms + EPS) * g_ref[...]).astype(BF16)

    o_ref[...] = jnp.dot(xn_ref[...], w_ref[...], preferred_element_type=F32)


def _inproj(x, gain, w, tm, tn):
    n, d = x.shape
    nc = w.shape[1]
    return pl.pallas_call(
        _inproj_body,
        grid=(n // tm, nc // tn),
        in_specs=[pl.BlockSpec((tm, d), lambda i, j: (i, 0)),
                  pl.BlockSpec((1, d), lambda i, j: (0, 0)),
                  pl.BlockSpec((d, tn), lambda i, j: (0, j))],
        out_specs=pl.BlockSpec((tm, tn), lambda i, j: (i, j)),
        out_shape=jax.ShapeDtypeStruct((n, nc), F32),
        scratch_shapes=[pltpu.VMEM((tm, d), BF16)],
        compiler_params=pltpu.CompilerParams(
            dimension_semantics=("parallel", "arbitrary"), vmem_limit_bytes=VMEM_LIMIT),
        name="inproj",
    )(x, gain, w)


def _pad_rows(x, rows):
    if x.shape[0] == rows:
        return x
    return jnp.concatenate([x, jnp.zeros((rows - x.shape[0],) + x.shape[1:], x.dtype)], axis=0)


def _gdn_body(qkv_ref, z_ref, ba_ref, cs_ref, s0_ref, cw_ref, alog_ref, dtb_ref, na_ref,
              o_ref, sout_ref, xbuf, s_scr, *, tin, nt):
    R = BLOCK_ROWS
    t = pl.program_id(1)

    @pl.when(t == 0)
    def _():
        xbuf[0:CONV_PAD, :] = cs_ref[0]
        s_scr[...] = s0_ref[0]

    if nt > 1:
        @pl.when(t > 0)
        def _():
            xbuf[0:CONV_PAD, :] = xbuf[tin:tin + CONV_PAD, :]

    xbuf[CONV_PAD:CONV_PAD + tin, :] = qkv_ref[...]
    if tin < R:
        xbuf[CONV_PAD + tin:CONV_PAD + R, :] = jnp.zeros((R - tin, xbuf.shape[1]), F32)

    rows = lax.broadcasted_iota(jnp.int32, (R, HEAD_DIM), 0)
    lanes = lax.broadcasted_iota(jnp.int32, (R, HEAD_DIM), 1)
    valid = rows < tin
    ti = lax.broadcasted_iota(jnp.int32, (R, R), 0)
    si = lax.broadcasted_iota(jnp.int32, (R, R), 1)
    causal = ti >= si
    merge_key = jnp.where(ti > si, ti ^ si, 0)

    ba = _pad_rows(ba_ref[...], R)
    sp_in = ba + dtb_ref[...]
    softplus = jnp.maximum(sp_in, 0.0) + jnp.log(1.0 + jnp.exp(-jnp.abs(sp_in)))
    g_all = jnp.where(valid, -jnp.exp(alog_ref[...]) * softplus, 0.0)
    beta_all = jnp.where(valid, _sigmoid(ba), 0.0)
    gcum_all = _hdot(causal.astype(F32), g_all)
    gate_src = jnp.where(lanes < N_HEADS, beta_all, gcum_all)

    sel_l = lax.broadcasted_iota(jnp.int32, (HEAD_DIM, 2 * HEAD_DIM), 0)
    sel_c = lax.broadcasted_iota(jnp.int32, (HEAD_DIM, 2 * HEAD_DIM), 1)
    lane0 = (lanes == 0).astype(F32)

    def head(h, s_old):
        off = h * HEAD_DIM

        def conv_silu(base):
            cols = pl.ds(pl.multiple_of(base + off, HEAD_DIM), HEAD_DIM)
            acc = None
            for j in range(CONV_WIDTH):
                term = xbuf[pl.ds(CONV_PAD - (CONV_WIDTH - 1) + j, R), cols] * cw_ref[j:j + 1, cols]
                acc = term if acc is None else acc + term
            return acc * _sigmoid(acc)

        qc = conv_silu(0)
        kc = conv_silu(D_HEADS)
        vc = conv_silu(2 * D_HEADS)
        qn = qc * lax.rsqrt(jnp.sum(qc * qc, axis=-1, keepdims=True) + EPS) * (HEAD_DIM ** -0.5)
        kn = jnp.where(valid, kc * lax.rsqrt(jnp.sum(kc * kc, axis=-1, keepdims=True) + EPS), 0.0)

        sel = (((sel_c < HEAD_DIM) & (sel_l == h)) |
               ((sel_c >= HEAD_DIM) & (sel_l == h + N_HEADS))).astype(F32)
        bg = _hdot(gate_src, sel)
        yield
        beta = bg[:, :HEAD_DIM]
        gc = bg[:, HEAD_DIM:]
        gc_last = gc[R - 1:R, :]
        gc_t = jnp.concatenate([gc, gc], axis=1)
        gc_s = lax.dot_general(lane0, gc, NT_DIMS, precision=HIGHEST, preferred_element_type=F32)
        yield
        decay = jnp.where(causal, jnp.exp(jnp.where(causal, gc_t - gc_s, 0.0)), 0.0)

        kb = kn * beta
        aq = _bdot_general(jnp.concatenate([kb, qn], axis=0), kn, NT_DIMS)
        yield
        a_mat = jnp.where(ti > si, aq[:R] * decay, 0.0)
        qk = aq[R:] * decay

        n_mat = None
        b = 1
        while b < R:
            c_m = jnp.where((merge_key >= b) & (merge_key < 2 * b), a_mat, 0.0)
            if n_mat is None:
                n_mat = -c_m
            else:
                p = c_m + _bdot(n_mat, c_m)
                yield
                n_mat = n_mat - p - _bdot(p, n_mat)
                yield
            b *= 2

        rhs = jnp.concatenate([vc * beta, kb * jnp.exp(gc)], axis=1)
        sol = rhs + _bdot(n_mat, rhs)
        yield
        u = sol[:, :HEAD_DIM]
        w = sol[:, HEAD_DIM:]

        qd = qn * jnp.exp(gc)
        kd = kn * jnp.exp(gc_last - gc)
        ws = _bdot(jnp.concatenate([w, qd], axis=0), s_old)
        yield
        v_new = u - ws[:R]
        o = ws[R:] + _bdot(qk, v_new)
        s_new = s_old * jnp.exp(gc_last) + _bdot_general(kd, v_new, TN_DIMS)
        yield

        cols = pl.ds(pl.multiple_of(off, HEAD_DIM), HEAD_DIM)
        on = o * lax.rsqrt(jnp.mean(o * o, axis=-1, keepdims=True) + EPS) * na_ref[...]
        z = z_ref[:, cols]
        o_ref[:, cols] = (on[:tin] * (z * _sigmoid(z))).astype(BF16)
        return s_new

    def head_group(hg, carry):
        heads = [hg * GDN_HEAD_GROUP + u for u in range(GDN_HEAD_GROUP)]
        states = _trace_round_robin([head(h, s_scr[h]) for h in heads])
        for h, s in zip(heads, states):
            s_scr[h] = s
        return carry

    lax.fori_loop(0, N_HEADS // GDN_HEAD_GROUP, head_group, 0)

    @pl.when(t == nt - 1)
    def _():
        sout_ref[0] = s_scr[...]


def _gdn(proj, ba, conv_state, s0, conv_w, alog_l, dtb_l, norm_a, *, batch, tin, nt, row0):
    rb0 = row0 // tin
    body = functools.partial(_gdn_body, tin=tin, nt=nt)
    return pl.pallas_call(
        body,
        grid=(batch, nt),
        in_specs=[
            pl.BlockSpec((tin, 3 * D_HEADS), lambda b, t: (rb0 + b * nt + t, 0)),
            pl.BlockSpec((tin, D_HEADS), lambda b, t: (rb0 + b * nt + t, 3)),
            pl.BlockSpec((tin, HEAD_DIM), lambda b, t: (rb0 + b * nt + t, 0)),
            pl.BlockSpec((1, CONV_PAD, 3 * D_HEADS), lambda b, t: (b, 0, 0)),
            pl.BlockSpec((1, N_HEADS, HEAD_DIM, HEAD_DIM), lambda b, t: (b, 0, 0, 0)),
            pl.BlockSpec((CONV_PAD, 3 * D_HEADS), lambda b, t: (0, 0)),
            pl.BlockSpec((1, HEAD_DIM), lambda b, t: (0, 0)),
            pl.BlockSpec((1, HEAD_DIM), lambda b, t: (0, 0)),
            pl.BlockSpec((1, HEAD_DIM), lambda b, t: (0, 0)),
        ],
        out_specs=[
            pl.BlockSpec((tin, D_HEADS), lambda b, t: (b * nt + t, 0)),
            pl.BlockSpec((1, N_HEADS, HEAD_DIM, HEAD_DIM), lambda b, t: (b, 0, 0, 0)),
        ],
        out_shape=[jax.ShapeDtypeStruct((batch * nt * tin, D_HEADS), BF16),
                   jax.ShapeDtypeStruct((batch, N_HEADS, HEAD_DIM, HEAD_DIM), F32)],
        scratch_shapes=[pltpu.VMEM((BLOCK_ROWS + CONV_PAD, 3 * D_HEADS), F32),
                        pltpu.VMEM((N_HEADS, HEAD_DIM, HEAD_DIM), F32)],
        compiler_params=pltpu.CompilerParams(
            dimension_semantics=("parallel", "arbitrary"), vmem_limit_bytes=VMEM_LIMIT),
        name="gated_delta",
    )(proj, proj, ba, conv_state, s0, conv_w, alog_l, dtb_l, norm_a)


def _gla_body(q_ref, f_ref, i_ref, z_ref, lbl_ref, nb_ref, s0_ref, o_ref, sout_ref,
              s_scr, *row_scr, tin, nt, layer):
    R = BLOCK_ROWS
    n_chunks = R // GLA_CHUNK
    t = pl.program_id(1)

    @pl.when(t == 0)
    def _():
        s_scr[...] = s0_ref[0]

    rows = lax.broadcasted_iota(jnp.int32, (R, HEAD_DIM), 0)
    valid = rows < tin
    ti = lax.broadcasted_iota(jnp.int32, (R, R), 0)
    si = lax.broadcasted_iota(jnp.int32, (R, R), 1)
    chunk_tri = ((ti >= si) & ((ti // GLA_CHUNK) == (si // GLA_CHUNK))).astype(F32)
    pair_key = jnp.where(ti > si, ti ^ si, 0)
    row_local = lax.broadcasted_iota(jnp.int32, (GLA_SUB, HEAD_DIM), 0)

    def head_pre(h, slot):
        q_scr, k_scr, v_scr, b_scr, _ = (r.at[slot] for r in row_scr)
        cols = pl.ds(pl.multiple_of(h * HEAD_DIM, HEAD_DIM), HEAD_DIM)
        lbl = lbl_ref[:, cols]
        lb_e = jnp.exp(lbl - jnp.max(lbl, axis=0, keepdims=True))
        lb = jnp.sum(lb_e[0:layer + 1], axis=0, keepdims=True) / jnp.sum(lb_e, axis=0, keepdims=True)

        fl = _pad_rows(f_ref[:, cols], R)
        qx = _pad_rows(q_ref[:, cols], R)
        vv = _pad_rows(i_ref[:, cols], R)
        log_f = jnp.where(valid, jnp.log(lb + (1.0 - lb) * _sigmoid(fl)), 0.0)
        kk = jnp.where(valid, (1.0 - lb) * _sigmoid(-fl), 0.0)
        qq = qx * _sigmoid(qx)
        bc = _hdot(chunk_tri, log_f)
        yield
        q_scr[...] = qq
        k_scr[...] = kk
        v_scr[...] = vv
        b_scr[...] = bc

        def block_row(period, row):
            return jnp.concatenate(
                [jnp.broadcast_to(b_scr[start + row:start + row + 1, :], (period, HEAD_DIM))
                 for start in range(0, R, period)], axis=0)

        b_last = block_row(GLA_CHUNK, GLA_CHUNK - 1)
        qd = qq * jnp.exp(bc)
        kd = kk * jnp.exp(b_last - bc)

        scores = jnp.zeros((R, R), F32)
        hs = GLA_SUB
        while hs < GLA_CHUNK:
            b_mid = block_row(2 * hs, hs - 1)
            lower = (rows & hs) != 0
            q_fac = jnp.where(lower, qq * jnp.exp(jnp.minimum(bc - b_mid, 0.0)), 0.0)
            k_fac = jnp.where(lower, 0.0, kk * jnp.exp(jnp.minimum(b_mid - bc, 0.0)))
            level = _bdot_general(q_fac, k_fac, NT_DIMS)
            yield
            scores = jnp.where((pair_key >= hs) & (pair_key < 2 * hs), level, scores)
            hs *= 2
        o_off = _bdot(scores, vv)
        yield
        return qd, kd, vv, o_off

    def diag_block(m, carry):
        r0 = pl.multiple_of(m * GLA_SUB, GLA_SUB)
        for slot in range(HEAD_GROUP):
            q_scr, k_scr, v_scr, b_scr, od_scr = (r.at[slot] for r in row_scr)
            q_blk = q_scr[pl.ds(r0, GLA_SUB), :]
            b_blk = b_scr[pl.ds(r0, GLA_SUB), :]
            acc = jnp.zeros((GLA_SUB, HEAD_DIM), F32)
            for s in range(GLA_SUB):
                b_s = b_scr[pl.ds(r0 + s, 1), :]
                k_s = k_scr[pl.ds(r0 + s, 1), :]
                v_s = v_scr[pl.ds(r0 + s, 1), :]
                e = jnp.where(row_local >= s, jnp.exp(jnp.minimum(b_blk - b_s, 0.0)), 0.0)
                col = jnp.sum(q_blk * k_s * e, axis=-1, keepdims=True)
                acc = acc + col * v_s
            od_scr[pl.ds(r0, GLA_SUB), :] = acc
        return carry

    def head_post(h, slot, pre, s_t):
        b_scr, od_scr = row_scr[3].at[slot], row_scr[4].at[slot]
        qd, kd, vv, o_off = pre
        o_inter = []
        for c in range(n_chunks):
            sl = slice(c * GLA_CHUNK, (c + 1) * GLA_CHUNK)
            o_inter.append(_bdot_general(qd[sl], s_t, NT_DIMS))
            f_last = jnp.exp(b_scr[(c + 1) * GLA_CHUNK - 1:(c + 1) * GLA_CHUNK, :])
            s_t = s_t * f_last + _bdot_general(vv[sl], kd[sl], TN_DIMS)
            yield

        o = jnp.concatenate(o_inter, axis=0) + o_off + od_scr[...]
        on = o * lax.rsqrt(jnp.mean(o * o, axis=-1, keepdims=True) + EPS) * nb_ref[...]
        cols = pl.ds(pl.multiple_of(h * HEAD_DIM, HEAD_DIM), HEAD_DIM)
        z = z_ref[:, cols]
        o_ref[:, cols] = (on[:tin] * _sigmoid(z)).astype(BF16)
        return s_t

    def head_group(hg, carry):
        heads = [hg * HEAD_GROUP + u for u in range(HEAD_GROUP)]
        states = [s_scr[h] for h in heads]
        pre = _trace_round_robin([head_pre(h, u) for u, h in enumerate(heads)])
        lax.fori_loop(0, R // GLA_SUB, diag_block, 0, unroll=DIAG_UNROLL)
        states = _trace_round_robin([head_post(h, u, p, s) for u, (h, p, s) in enumerate(zip(heads, pre, states))])
        for h, s in zip(heads, states):
            s_scr[h] = s
        return carry

    lax.fori_loop(0, N_HEADS // HEAD_GROUP, head_group, 0)

    @pl.when(t == nt - 1)
    def _():
        sout_ref[0] = s_scr[...]


def _gla(proj, lb_logits, norm_b, s0_t, *, batch, tin, nt, row0, layer):
    rb0 = row0 // tin
    body = functools.partial(_gla_body, tin=tin, nt=nt, layer=layer)
    col_spec = lambda c: pl.BlockSpec((tin, D_HEADS), lambda b, t: (rb0 + b * nt + t, c))
    head_rows = pltpu.VMEM((HEAD_GROUP, BLOCK_ROWS, HEAD_DIM), F32)
    return pl.pallas_call(
        body,
        grid=(batch, nt),
        in_specs=[
            col_spec(4), col_spec(5), col_spec(6), col_spec(7),
            pl.BlockSpec(lb_logits.shape, lambda b, t: (0, 0)),
            pl.BlockSpec((1, HEAD_DIM), lambda b, t: (0, 0)),
            pl.BlockSpec((1, N_HEADS, HEAD_DIM, HEAD_DIM), lambda b, t: (b, 0, 0, 0)),
        ],
        out_specs=[
            pl.BlockSpec((tin, D_HEADS), lambda b, t: (b * nt + t, 0)),
            pl.BlockSpec((1, N_HEADS, HEAD_DIM, HEAD_DIM), lambda b, t: (b, 0, 0, 0)),
        ],
        out_shape=[jax.ShapeDtypeStruct((batch * nt * tin, D_HEADS), BF16),
                   jax.ShapeDtypeStruct((batch, N_HEADS, HEAD_DIM, HEAD_DIM), F32)],
        scratch_shapes=[pltpu.VMEM((N_HEADS, HEAD_DIM, HEAD_DIM), F32),
                        head_rows, head_rows, head_rows, head_rows, head_rows],
        compiler_params=pltpu.CompilerParams(
            dimension_semantics=("parallel", "arbitrary"), vmem_limit_bytes=VMEM_LIMIT),
        name="hgrn2",
    )(proj, proj, proj, proj, lb_logits, norm_b, s0_t)


def _merge_body(oa_ref, ob_ref, g0_ref, g1_ref, x_ref, wb0_ref, wb1_ref, wo_ref, nf_ref, wr_ref, br_ref,
                x1_ref, h_ref, lg_ref):
    ua = jnp.dot(oa_ref[...], wb0_ref[...], preferred_element_type=F32)
    ub = jnp.dot(ob_ref[...], wb1_ref[...], preferred_element_type=F32)
    merged = _sigmoid(g0_ref[...]) * ua + _sigmoid(g1_ref[...]) * ub
    x1 = x_ref[...] + jnp.dot(merged.astype(BF16), wo_ref[...], preferred_element_type=F32)
    x1_ref[...] = x1
    hn = x1 * lax.rsqrt(jnp.mean(x1 * x1, axis=-1, keepdims=True) + EPS) * nf_ref[...]
    h_ref[...] = hn
    lg_ref[...] = _hdot(hn, wr_ref[...]) + br_ref[...]


def _merge(o_a, o_b, proj, x, wb0, wb1, wo, norm_ffn, w_router, b_router, tm):
    n, d = x.shape
    const = lambda shape: pl.BlockSpec(shape, lambda i: (0, 0), pipeline_mode=pl.Buffered(1))
    return pl.pallas_call(
        _merge_body,
        grid=(n // tm,),
        in_specs=[
            pl.BlockSpec((tm, D_HEADS), lambda i: (i, 0)),
            pl.BlockSpec((tm, D_HEADS), lambda i: (i, 0)),
            pl.BlockSpec((tm, d), lambda i: (i, 8)),
            pl.BlockSpec((tm, d), lambda i: (i, 9)),
            pl.BlockSpec((tm, d), lambda i: (i, 0)),
            const(wb0.shape), const(wb1.shape), const(wo.shape),
            const((1, d)), const(w_router.shape), const((1, HEAD_DIM)),
        ],
        out_specs=[pl.BlockSpec((tm, d), lambda i: (i, 0)),
                   pl.BlockSpec((tm, d), lambda i: (i, 0)),
                   pl.BlockSpec((tm, HEAD_DIM), lambda i: (i, 0))],
        out_shape=[jax.ShapeDtypeStruct((n, d), F32),
                   jax.ShapeDtypeStruct((n, d), F32),
                   jax.ShapeDtypeStruct((n, HEAD_DIM), F32)],
        compiler_params=pltpu.CompilerParams(
            dimension_semantics=("parallel",), vmem_limit_bytes=VMEM_LIMIT),
        name="merge_out",
    )(o_a, o_b, proj, proj, x, wb0, wb1, wo, norm_ffn, w_router, b_router)


def _moe_body(be_ref, nu_ref, x_ref, wg_ref, wu_ref, bg_ref, bu_ref, wd_ref, bd_ref, o_ref):
    i = pl.program_id(0)
    f = pl.program_id(1)

    @pl.when(i < nu_ref[0])
    def _():
        x = x_ref[...].astype(BF16)
        gate = jnp.dot(x, wg_ref[0], preferred_element_type=F32) + bg_ref[0]
        up = jnp.dot(x, wu_ref[0], preferred_element_type=F32) + bu_ref[0]
        gate = jnp.minimum(gate, SWIGLU_LIMIT)
        up = jnp.clip(up, -SWIGLU_LIMIT, SWIGLU_LIMIT)
        act = (up + 1.0) * gate * _sigmoid(SWIGLU_ALPHA * gate)
        part = jnp.dot(act.astype(BF16), wd_ref[0], preferred_element_type=F32)

        @pl.when(f == 0)
        def _():
            o_ref[...] = part + bd_ref[0]

        @pl.when(f > 0)
        def _():
            o_ref[...] += part


def _moe(block_expert, n_used, xs, w_gu, b_gu, w_d, b_d):
    n_rows, d = xs.shape
    n_exp, _, two_ff = w_gu.shape
    d_ff = two_ff // 2
    tf = min(MOE_FF_TILE, d_ff)
    nf = d_ff // tf
    nblk = n_rows // MOE_ROWS
    grid_spec = pltpu.PrefetchScalarGridSpec(
        num_scalar_prefetch=2,
        grid=(nblk, nf),
        in_specs=[
            pl.BlockSpec((MOE_ROWS, d), lambda i, f, be, nu: (i, 0)),
            pl.BlockSpec((1, d, tf), lambda i, f, be, nu: (be[i], 0, f)),
            pl.BlockSpec((1, d, tf), lambda i, f, be, nu: (be[i], 0, nf + f)),
            pl.BlockSpec((1, 1, tf), lambda i, f, be, nu: (be[i], 0, f)),
            pl.BlockSpec((1, 1, tf), lambda i, f, be, nu: (be[i], 0, nf + f)),
            pl.BlockSpec((1, tf, d), lambda i, f, be, nu: (be[i], f, 0)),
            pl.BlockSpec((1, 1, d), lambda i, f, be, nu: (be[i], 0, 0)),
        ],
        out_specs=pl.BlockSpec((MOE_ROWS, d), lambda i, f, be, nu: (i, 0)),
    )
    return pl.pallas_call(
        _moe_body,
        grid_spec=grid_spec,
        out_shape=jax.ShapeDtypeStruct((n_rows, d), F32),
        compiler_params=pltpu.CompilerParams(
            dimension_semantics=("parallel", "arbitrary"), vmem_limit_bytes=VMEM_LIMIT),
        name="expert_mlp",
    )(block_expert, n_used, xs, w_gu, w_gu, b_gu, b_gu, w_d, b_d)


def _row_copy(src_ref, src_row, dst_ref, dst_row, sem):
    return pltpu.make_async_copy(src_ref.at[pl.ds(src_row, 1)], dst_ref.at[pl.ds(dst_row, 1)], sem)


def _dispatch_body(dest_ref, h_ref, xs_in_ref, xs_ref, sem, *, tm):
    del xs_in_ref

    def start(r, carry):
        for k in range(TOP_K):
            _row_copy(h_ref, r, xs_ref, dest_ref[0, 0, r * TOP_K + k], sem).start()
        return carry

    def wait(r, carry):
        for _ in range(TOP_K):
            _row_copy(h_ref, 0, xs_ref, 0, sem).wait()
        return carry

    lax.fori_loop(0, tm, start, 0, unroll=ROW_COPY_UNROLL)
    lax.fori_loop(0, tm, wait, 0, unroll=ROW_COPY_UNROLL)


def _dispatch(dest, h, n_rows, tm):
    n, d = h.shape
    xs0 = jnp.zeros((n_rows, d), F32)
    return pl.pallas_call(
        functools.partial(_dispatch_body, tm=tm),
        grid=(n // tm,),
        in_specs=[pl.BlockSpec((1, 1, tm * TOP_K), lambda i: (i, 0, 0), memory_space=pltpu.SMEM),
                  pl.BlockSpec((tm, d), lambda i: (i, 0)),
                  pl.BlockSpec(memory_space=pl.ANY)],
        out_specs=pl.BlockSpec(memory_space=pl.ANY),
        out_shape=jax.ShapeDtypeStruct((n_rows, d), F32),
        scratch_shapes=[pltpu.SemaphoreType.DMA(())],
        input_output_aliases={2: 0},
        compiler_params=pltpu.CompilerParams(
            dimension_semantics=("arbitrary",), vmem_limit_bytes=VMEM_LIMIT),
        name="dispatch_rows",
    )(dest.reshape(n // tm, 1, tm * TOP_K), h, xs0)


def _combine_body(dest_ref, gate_ref, x_ref, g_ref, ys_ref, o_ref, ybuf, sem, *, tm, apply_norm):
    def start(r, carry):
        for k in range(TOP_K):
            _row_copy(ys_ref, dest_ref[0, 0, r * TOP_K + k], ybuf.at[k], r, sem).start()
        return carry

    def wait(r, carry):
        for k in range(TOP_K):
            _row_copy(ys_ref, 0, ybuf.at[k], 0, sem).wait()
        return carry

    lax.fori_loop(0, tm, start, 0, unroll=ROW_COPY_UNROLL)
    lax.fori_loop(0, tm, wait, 0, unroll=ROW_COPY_UNROLL)

    gate = gate_ref[...]
    y = gate[:, 0:1] * ybuf[0]
    for k in range(1, TOP_K):
        y = y + gate[:, k:k + 1] * ybuf[k]
    x = x_ref[...] + y
    if apply_norm:
        x = x * lax.rsqrt(jnp.mean(x * x, axis=-1, keepdims=True) + EPS) * g_ref[...]
    o_ref[...] = x


def _combine(dest, gate, x1, gain, ys, tm, *, apply_norm):
    n, d = x1.shape
    return pl.pallas_call(
        functools.partial(_combine_body, tm=tm, apply_norm=apply_norm),
        grid=(n // tm,),
        in_specs=[pl.BlockSpec((1, 1, tm * TOP_K), lambda i: (i, 0, 0), memory_space=pltpu.SMEM),
                  pl.BlockSpec((tm, TOP_K), lambda i: (i, 0)),
                  pl.BlockSpec((tm, d), lambda i: (i, 0)),
                  pl.BlockSpec((1, d), lambda i: (0, 0)),
                  pl.BlockSpec(memory_space=pl.ANY)],
        out_specs=pl.BlockSpec((tm, d), lambda i: (i, 0)),
        out_shape=jax.ShapeDtypeStruct((n, d), F32),
        scratch_shapes=[pltpu.VMEM((TOP_K, tm, d), F32), pltpu.SemaphoreType.DMA(())],
        compiler_params=pltpu.CompilerParams(
            dimension_semantics=("arbitrary",), vmem_limit_bytes=VMEM_LIMIT),
        name="combine_final_norm",
    )(dest.reshape(n // tm, 1, tm * TOP_K), gate, x1, gain, ys)


def _token_tile(n, candidates):
    for c in candidates:
        if n % c == 0:
            return c
    raise ValueError(f"token count {n} has no tile in {candidates}")


def _lane_vector(values, first_lane):
    out = jnp.zeros((1, HEAD_DIM), F32)
    return lax.dynamic_update_slice(out, values.reshape(1, -1).astype(F32), (0, first_lane))


def _route(logits, n_exp):
    n_tok = logits.shape[0]
    top_logit, top_idx = lax.top_k(logits, TOP_K)
    gate = jax.nn.softmax(top_logit, axis=-1)
    n_assign = n_tok * TOP_K
    e_flat = top_idx.reshape(-1)
    onehot = (e_flat[:, None] == jnp.arange(n_exp)[None, :]).astype(jnp.int32)
    seen = jnp.cumsum(onehot, axis=0)
    rank = jnp.sum(onehot * seen, axis=1) - 1
    counts = seen[-1]
    padded = ((counts + MOE_ROWS - 1) // MOE_ROWS) * MOE_ROWS
    pend = jnp.cumsum(padded)
    pstart = pend - padded
    dest = (pstart[e_flat] + rank).astype(jnp.int32)
    n_blocks = -(-n_assign // MOE_ROWS) + n_exp
    block_expert = jnp.minimum(
        jnp.searchsorted(pend, jnp.arange(n_blocks) * MOE_ROWS, side='right'), n_exp - 1).astype(jnp.int32)
    n_used = (pend[-1] // MOE_ROWS).astype(jnp.int32).reshape(1)
    return gate, dest, block_expert, n_used, n_blocks * MOE_ROWS


def kernel(x_prompt, x_sample, state_conv_a, state_delta, state_hgrn, norm_mix, w_in, conv_a, a_log, dt_bias,
           norm_a, lb_logits, norm_b, w_branch, w_out, norm_ffn, w_router, b_router, w_gate_up, b_gate_up,
           w_down, b_down, norm_final):
    bp, tp, d = x_prompt.shape
    bs, ts, _ = x_sample.shape
    depth = w_in.shape[0]
    n_exp = w_router.shape[-1]
    n_p, n_s = bp * tp, bs * ts
    n = n_p + n_s
    assert d == D_HEADS and tp % BLOCK_ROWS == 0 and ts <= BLOCK_ROWS and n_p % ts == 0
    assert min(tp, ts) >= CONV_WIDTH - 1
    d_conv = 3 * D_HEADS
    tm_big = _token_tile(n, (768, 512, 256))
    tm_small = _token_tile(n, (256,))

    x = jnp.concatenate([x_prompt.reshape(n_p, d), x_sample.reshape(n_s, d)], axis=0)
    conv_p, delta_p, hgrn_p, conv_s, delta_s, hgrn_s = [], [], [], [], [], []
    for l in range(depth):
        w_l = w_in[l]
        c0 = d_conv
        w_main = jnp.concatenate([w_l[:, :c0], w_l[:, c0 + 2 * N_HEADS:]], axis=1).astype(BF16)
        w_ba = jnp.pad(w_l[:, c0:c0 + 2 * N_HEADS], ((0, 0), (0, HEAD_DIM - 2 * N_HEADS))).astype(BF16)
        gain = norm_mix[l].reshape(1, d)
        proj = _inproj(x, gain, w_main, tm_big, 1024)
        ba = _inproj(x, gain, w_ba, tm_big, HEAD_DIM)

        conv_w = jnp.pad(conv_a[l], ((0, CONV_PAD - CONV_WIDTH), (0, 0)))
        alog_l = _lane_vector(a_log[l], N_HEADS)
        dtb_l = _lane_vector(dt_bias[l], N_HEADS)
        na = norm_a[l].reshape(1, HEAD_DIM)
        nb = norm_b[l].reshape(1, HEAD_DIM)
        lbl = jnp.pad(lb_logits.astype(F32), ((0, CONV_PAD - lb_logits.shape[0]), (0, 0)), constant_values=-1e30)

        zero_cs = jnp.zeros((bp, CONV_PAD, d_conv), F32)
        zero_s = jnp.zeros((bp, N_HEADS, HEAD_DIM, HEAD_DIM), F32)
        cs_s = jnp.pad(state_conv_a[l], ((0, 0), (CONV_PAD - (CONV_WIDTH - 1), 0), (0, 0)))

        oa_p, sd_p = _gdn(proj, ba, zero_cs, zero_s, conv_w, alog_l, dtb_l, na,
                          batch=bp, tin=BLOCK_ROWS, nt=tp // BLOCK_ROWS, row0=0)
        oa_s, sd_s = _gdn(proj, ba, cs_s, state_delta[l], conv_w, alog_l, dtb_l, na,
                          batch=bs, tin=ts, nt=1, row0=n_p)
        ob_p, sh_p = _gla(proj, lbl, nb, zero_s, batch=bp, tin=BLOCK_ROWS, nt=tp // BLOCK_ROWS, row0=0, layer=l)
        ob_s, sh_s = _gla(proj, lbl, nb, jnp.swapaxes(state_hgrn[l], -1, -2),
                          batch=bs, tin=ts, nt=1, row0=n_p, layer=l)
        o_a = jnp.concatenate([oa_p, oa_s], axis=0)
        o_b = jnp.concatenate([ob_p, ob_s], axis=0)

        tail = CONV_WIDTH - 1
        conv_p.append(jnp.stack([proj[(b + 1) * tp - tail:(b + 1) * tp, :d_conv] for b in range(bp)]))
        conv_s.append(jnp.stack([proj[n_p + (b + 1) * ts - tail:n_p + (b + 1) * ts, :d_conv] for b in range(bs)]))
        delta_p.append(sd_p)
        delta_s.append(sd_s)
        hgrn_p.append(jnp.swapaxes(sh_p, -1, -2))
        hgrn_s.append(jnp.swapaxes(sh_s, -1, -2))

        w_r = jnp.pad(w_router[l], ((0, 0), (0, HEAD_DIM - n_exp)))
        b_r = jnp.pad(b_router[l], (0, HEAD_DIM - n_exp)).reshape(1, HEAD_DIM)
        x1, h, logits = _merge(o_a, o_b, proj, x, w_branch[l, 0].astype(BF16), w_branch[l, 1].astype(BF16),
                               w_out[l].astype(BF16), norm_ffn[l].reshape(1, d), w_r, b_r, tm_small)

        gate, dest, block_expert, n_used, n_rows = _route(logits[:, :n_exp], n_exp)
        xs = _dispatch(dest, h, n_rows, tm_small)
        ys = _moe(block_expert, n_used, xs, w_gate_up[l].astype(BF16), b_gate_up[l][:, None, :],
                  w_down[l].astype(BF16), b_down[l][:, None, :])
        last = l + 1 == depth
        x = _combine(dest, gate, x1, norm_final.reshape(1, d), ys, tm_small, apply_norm=last)
    y_prompt = x[:n_p].reshape(bp, tp, d)
    y_sample = x[n_p:].reshape(bs, ts, d)
    return (y_prompt, y_sample, jnp.stack(conv_p), jnp.stack(delta_p), jnp.stack(hgrn_p),
            jnp.stack(conv_s), jnp.stack(delta_s), jnp.stack(hgrn_s))
```

```python
import functools

import jax
import jax.numpy as jnp
from jax import lax
from jax.experimental import pallas as pl
from jax.experimental.pallas import tpu as pltpu

F32 = jnp.float32
BF16 = jnp.bfloat16

EPS = 1e-6
HEAD_DIM = 128
N_HEADS = 16
D_HEADS = N_HEADS * HEAD_DIM
CONV_WIDTH = 4
CONV_PAD = 8
BLOCK_ROWS = 256
GLA_CHUNK = 64
GLA_SUB = 8
HEAD_GROUP = 2
GDN_HEAD_GROUP = 4
DIAG_UNROLL = True
TOP_K = 4
SWIGLU_LIMIT = 7.0
SWIGLU_ALPHA = 1.702
MOE_ROWS = 512
MOE_FF_TILE = 1024
ROW_COPY_UNROLL = 2
VMEM_LIMIT = 56 * 1024 * 1024

NT_DIMS = (((1,), (1,)), ((), ()))
TN_DIMS = (((0,), (0,)), ((), ()))


def _sigmoid(x):
    return 1.0 / (1.0 + jnp.exp(-x))


def _bdot(a, b):
    return jnp.dot(a.astype(BF16), b.astype(BF16), preferred_element_type=F32)


def _bdot_general(a, b, dims):
    return lax.dot_general(a.astype(BF16), b.astype(BF16), dims, preferred_element_type=F32)


def _split3(x):
    hi = x.astype(BF16)
    r1 = x - hi.astype(F32)
    mid = r1.astype(BF16)
    lo = (r1 - mid.astype(F32)).astype(BF16)
    return hi, mid, lo


def _dot_x01(x, m01):
    m = m01.astype(BF16)
    return jnp.dot(jnp.concatenate(_split3(x), axis=1), jnp.concatenate([m, m, m], axis=0),
                   preferred_element_type=F32)


def _dot_01x(m01, x):
    m = m01.astype(BF16)
    return jnp.dot(jnp.concatenate([m, m, m], axis=1), jnp.concatenate(_split3(x), axis=0),
                   preferred_element_type=F32)


def _dot_01xt(m01, x):
    m = m01.astype(BF16)
    return lax.dot_general(jnp.concatenate([m, m, m], axis=1), jnp.concatenate(_split3(x), axis=1), NT_DIMS,
                           preferred_element_type=F32)


def _trace_round_robin(gens):
    results = [None] * len(gens)
    live = list(range(len(gens)))
    while live:
        for i in list(live):
            try:
                next(gens[i])
            except StopIteration as stop:
                results[i] = stop.value
                live.remove(i)
    return results


def _inproj_body(x_ref, g_ref, w_ref, o_ref, xn_ref):
    @pl.when(pl.program_id(1) == 0)
    def _():
        x = x_ref[...]
        ms = jnp.mean(x * x, axis=-1, keepdims=True)
        xn_ref[...] = (x * lax.rsqrt(ms + EPS) * g_ref[...]).astype(BF16)

    o_ref[...] = jnp.dot(xn_ref[...], w_ref[...], preferred_element_type=F32)


def _inproj(x, gain, w, tm, tn):
    n, d = x.shape
    nc = w.shape[1]
    return pl.pallas_call(
        _inproj_body,
        grid=(n // tm, nc // tn),
        in_specs=[pl.BlockSpec((tm, d), lambda i, j: (i, 0)),
                  pl.BlockSpec((1, d), lambda i, j: (0, 0)),
                  pl.BlockSpec((d, tn), lambda i, j: (0, j))],
        out_specs=pl.BlockSpec((tm, tn), lambda i, j: (i, j)),
        out_shape=jax.ShapeDtypeStruct((n, nc), F32),
        scratch_shapes=[pltpu.VMEM((tm, d), BF16)],
        compiler_params=pltpu.CompilerParams(
            dimension_semantics=("parallel", "arbitrary"), vmem_limit_bytes=VMEM_LIMIT),
        name="inproj",
    )(x, gain, w)


def _pad_rows(x, rows):
    if x.shape[0] == rows:
        return x
    return jnp.concatenate([x, jnp.zeros((rows - x.shape[0],) + x.shape[1:], x.dtype)], axis=0)


def _gdn_body(qkv_ref, z_ref, ba_ref, cs_ref, s0_ref, cw_ref, alog_ref, dtb_ref, na_ref,
              o_ref, sout_ref, xbuf, s_scr, *, tin, nt):
    R = BLOCK_ROWS
    t = pl.program_id(1)

    @pl.when(t == 0)
    def _():
        xbuf[0:CONV_PAD, :] = cs_ref[0]
        s_scr[...] = s0_ref[0]

    if nt > 1:
        @pl.when(t > 0)
        def _():
            xbuf[0:CONV_PAD, :] = xbuf[tin:tin + CONV_PAD, :]

    xbuf[CONV_PAD:CONV_PAD + tin, :] = qkv_ref[...]
    if tin < R:
        xbuf[CONV_PAD + tin:CONV_PAD + R, :] = jnp.zeros((R - tin, xbuf.shape[1]), F32)

    rows = lax.broadcasted_iota(jnp.int32, (R, HEAD_DIM), 0)
    lanes = lax.broadcasted_iota(jnp.int32, (R, HEAD_DIM), 1)
    valid = rows < tin
    ti = lax.broadcasted_iota(jnp.int32, (R, R), 0)
    si = lax.broadcasted_iota(jnp.int32, (R, R), 1)
    causal = ti >= si
    merge_key = jnp.where(ti > si, ti ^ si, 0)

    ba = _pad_rows(ba_ref[...], R)
    sp_in = ba + dtb_ref[...]
    softplus = jnp.maximum(sp_in, 0.0) + jnp.log(1.0 + jnp.exp(-jnp.abs(sp_in)))
    g_all = jnp.where(valid, -jnp.exp(alog_ref[...]) * softplus, 0.0)
    beta_all = jnp.where(valid, _sigmoid(ba), 0.0)
    gcum_all = _dot_01x(causal, g_all)
    gate_src = jnp.where(lanes < N_HEADS, beta_all, gcum_all)
    HALF = R // 2
    half_key = merge_key[:HALF, :HALF]

    sel_l = lax.broadcasted_iota(jnp.int32, (HEAD_DIM, 2 * HEAD_DIM), 0)
    sel_c = lax.broadcasted_iota(jnp.int32, (HEAD_DIM, 2 * HEAD_DIM), 1)
    lane0 = (lanes == 0).astype(F32)

    def head(h, s_old):
        off = h * HEAD_DIM

        def conv_silu(base):
            cols = pl.ds(pl.multiple_of(base + off, HEAD_DIM), HEAD_DIM)
            acc = None
            for j in range(CONV_WIDTH):
                term = xbuf[pl.ds(CONV_PAD - (CONV_WIDTH - 1) + j, R), cols] * cw_ref[j:j + 1, cols]
                acc = term if acc is None else acc + term
            return acc * _sigmoid(acc)

        qc = conv_silu(0)
        kc = conv_silu(D_HEADS)
        vc = conv_silu(2 * D_HEADS)
        qn = qc * lax.rsqrt(jnp.sum(qc * qc, axis=-1, keepdims=True) + EPS) * (HEAD_DIM ** -0.5)
        kn = jnp.where(valid, kc * lax.rsqrt(jnp.sum(kc * kc, axis=-1, keepdims=True) + EPS), 0.0)

        sel = (((sel_c < HEAD_DIM) & (sel_l == h)) |
               ((sel_c >= HEAD_DIM) & (sel_l == h + N_HEADS))).astype(F32)
        bg = _dot_x01(gate_src, sel)
        yield
        beta = bg[:, :HEAD_DIM]
        gc = bg[:, HEAD_DIM:]
        gc_last = gc[R - 1:R, :]
        gc_t = jnp.concatenate([gc, gc], axis=1)
        gc_s = _dot_01xt(lane0, gc)
        yield
        decay = jnp.where(causal, jnp.exp(jnp.where(causal, gc_t - gc_s, 0.0)), 0.0)

        kb = kn * beta
        aq = _bdot_general(jnp.concatenate([kb, qn], axis=0), kn, NT_DIMS)
        yield
        a_mat = jnp.where(ti > si, aq[:R] * decay, 0.0)
        qk = aq[R:] * decay

        a_top, a_bot, a_cross = a_mat[:HALF, :HALF], a_mat[HALF:, HALF:], a_mat[HALF:, :HALF]
        n_top = n_bot = None
        b = 1
        while b < HALF:
            level = (half_key >= b) & (half_key < 2 * b)
            c_top = jnp.where(level, a_top, 0.0)
            c_bot = jnp.where(level, a_bot, 0.0)
            if n_top is None:
                n_top, n_bot = -c_top, -c_bot
            else:
                p_top = c_top + _bdot(n_top, c_top)
                p_bot = c_bot + _bdot(n_bot, c_bot)
                yield
                n_top = n_top - p_top - _bdot(p_top, n_top)
                n_bot = n_bot - p_bot - _bdot(p_bot, n_bot)
                yield
            b *= 2
        p_cross = a_cross + _bdot(n_bot, a_cross)
        yield
        n_cross = -(p_cross + _bdot(p_cross, n_top))
        yield

        rhs = jnp.concatenate([vc * beta, kb * jnp.exp(gc)], axis=1)
        rhs_top, rhs_bot = rhs[:HALF], rhs[HALF:]
        sol = jnp.concatenate([rhs_top + _bdot(n_top, rhs_top),
                               rhs_bot + _bdot(n_cross, rhs_top) + _bdot(n_bot, rhs_bot)], axis=0)
        yield
        u = sol[:, :HEAD_DIM]
        w = sol[:, HEAD_DIM:]

        qd = qn * jnp.exp(gc)
        kd = kn * jnp.exp(gc_last - gc)
        ws = _bdot(jnp.concatenate([w, qd], axis=0), s_old)
        yield
        v_new = u - ws[:R]
        o = ws[R:] + _bdot(qk, v_new)
        s_new = s_old * jnp.exp(gc_last) + _bdot_general(kd, v_new, TN_DIMS)
        yield

        cols = pl.ds(pl.multiple_of(off, HEAD_DIM), HEAD_DIM)
        on = o * lax.rsqrt(jnp.mean(o * o, axis=-1, keepdims=True) + EPS) * na_ref[...]
        z = z_ref[:, cols]
        o_ref[:, cols] = (on[:tin] * (z * _sigmoid(z))).astype(BF16)
        return s_new

    def head_group(hg, carry):
        heads = [hg * GDN_HEAD_GROUP + u for u in range(GDN_HEAD_GROUP)]
        states = _trace_round_robin([head(h, s_scr[h]) for h in heads])
        for h, s in zip(heads, states):
            s_scr[h] = s
        return carry

    lax.fori_loop(0, N_HEADS // GDN_HEAD_GROUP, head_group, 0)

    @pl.when(t == nt - 1)
    def _():
        sout_ref[0] = s_scr[...]


def _gdn(proj, ba, conv_state, s0, conv_w, alog_l, dtb_l, norm_a, *, batch, tin, nt, row0):
    rb0 = row0 // tin
    body = functools.partial(_gdn_body, tin=tin, nt=nt)
    return pl.pallas_call(
        body,
        grid=(batch, nt),
        in_specs=[
            pl.BlockSpec((tin, 3 * D_HEADS), lambda b, t: (rb0 + b * nt + t, 0)),
            pl.BlockSpec((tin, D_HEADS), lambda b, t: (rb0 + b * nt + t, 3)),
            pl.BlockSpec((tin, HEAD_DIM), lambda b, t: (rb0 + b * nt + t, 0)),
            pl.BlockSpec((1, CONV_PAD, 3 * D_HEADS), lambda b, t: (b, 0, 0)),
            pl.BlockSpec((1, N_HEADS, HEAD_DIM, HEAD_DIM), lambda b, t: (b, 0, 0, 0)),
            pl.BlockSpec((CONV_PAD, 3 * D_HEADS), lambda b, t: (0, 0)),
            pl.BlockSpec((1, HEAD_DIM), lambda b, t: (0, 0)),
            pl.BlockSpec((1, HEAD_DIM), lambda b, t: (0, 0)),
            pl.BlockSpec((1, HEAD_DIM), lambda b, t: (0, 0)),
        ],
        out_specs=[
            pl.BlockSpec((tin, D_HEADS), lambda b, t: (b * nt + t, 0)),
            pl.BlockSpec((1, N_HEADS, HEAD_DIM, HEAD_DIM), lambda b, t: (b, 0, 0, 0)),
        ],
        out_shape=[jax.ShapeDtypeStruct((batch * nt * tin, D_HEADS), BF16),
                   jax.ShapeDtypeStruct((batch, N_HEADS, HEAD_DIM, HEAD_DIM), F32)],
        scratch_shapes=[pltpu.VMEM((BLOCK_ROWS + CONV_PAD, 3 * D_HEADS), F32),
                        pltpu.VMEM((N_HEADS, HEAD_DIM, HEAD_DIM), F32)],
        compiler_params=pltpu.CompilerParams(
            dimension_semantics=("parallel", "arbitrary"), vmem_limit_bytes=VMEM_LIMIT),
        name="gated_delta",
    )(proj, proj, ba, conv_state, s0, conv_w, alog_l, dtb_l, norm_a)


def _gla_body(q_ref, f_ref, i_ref, z_ref, lbl_ref, nb_ref, s0_ref, o_ref, sout_ref,
              s_scr, *row_scr, tin, nt, layer):
    R = BLOCK_ROWS
    n_chunks = R // GLA_CHUNK
    t = pl.program_id(1)

    @pl.when(t == 0)
    def _():
        s_scr[...] = s0_ref[0]

    rows = lax.broadcasted_iota(jnp.int32, (R, HEAD_DIM), 0)
    valid = rows < tin
    ti = lax.broadcasted_iota(jnp.int32, (R, R), 0)
    si = lax.broadcasted_iota(jnp.int32, (R, R), 1)
    chunk_tri = ((ti >= si) & ((ti // GLA_CHUNK) == (si // GLA_CHUNK))).astype(F32)
    pair_key = jnp.where(ti > si, ti ^ si, 0)
    row_local = lax.broadcasted_iota(jnp.int32, (GLA_SUB, HEAD_DIM), 0)

    def head_pre(h, slot):
        q_scr, k_scr, v_scr, b_scr, _ = (r.at[slot] for r in row_scr)
        cols = pl.ds(pl.multiple_of(h * HEAD_DIM, HEAD_DIM), HEAD_DIM)
        lbl = lbl_ref[:, cols]
        lb_e = jnp.exp(lbl - jnp.max(lbl, axis=0, keepdims=True))
        lb = jnp.sum(lb_e[0:layer + 1], axis=0, keepdims=True) / jnp.sum(lb_e, axis=0, keepdims=True)

        fl = _pad_rows(f_ref[:, cols], R)
        qx = _pad_rows(q_ref[:, cols], R)
        vv = _pad_rows(i_ref[:, cols], R)
        log_f = jnp.where(valid, jnp.log(lb + (1.0 - lb) * _sigmoid(fl)), 0.0)
        kk = jnp.where(valid, (1.0 - lb) * _sigmoid(-fl), 0.0)
        qq = qx * _sigmoid(qx)
        bc = _dot_01x(chunk_tri, log_f)
        yield
        q_scr[...] = qq
        k_scr[...] = kk
        v_scr[...] = vv
        b_scr[...] = bc

        def block_row(period, row):
            return jnp.concatenate(
                [jnp.broadcast_to(b_scr[start + row:start + row + 1, :], (period, HEAD_DIM))
                 for start in range(0, R, period)], axis=0)

        b_last = block_row(GLA_CHUNK, GLA_CHUNK - 1)
        qd = qq * jnp.exp(bc)
        kd = kk * jnp.exp(b_last - bc)

        scores = jnp.zeros((R, R), F32)
        hs = GLA_SUB
        while hs < GLA_CHUNK:
            b_mid = block_row(2 * hs, hs - 1)
            lower = (rows & hs) != 0
            q_fac = jnp.where(lower, qq * jnp.exp(jnp.minimum(bc - b_mid, 0.0)), 0.0)
            k_fac = jnp.where(lower, 0.0, kk * jnp.exp(jnp.minimum(b_mid - bc, 0.0)))
            level = _bdot_general(q_fac, k_fac, NT_DIMS)
            yield
            scores = jnp.where((pair_key >= hs) & (pair_key < 2 * hs), level, scores)
            hs *= 2
        o_off = _bdot(scores, vv)
        yield
        return qd, kd, vv, o_off

    def diag_block(m, carry):
        r0 = pl.multiple_of(m * GLA_SUB, GLA_SUB)
        for slot in range(HEAD_GROUP):
            q_scr, k_scr, v_scr, b_scr, od_scr = (r.at[slot] for r in row_scr)
            q_blk = q_scr[pl.ds(r0, GLA_SUB), :]
            b_blk = b_scr[pl.ds(r0, GLA_SUB), :]
            acc = jnp.zeros((GLA_SUB, HEAD_DIM), F32)
            for s in range(GLA_SUB):
                b_s = b_scr[pl.ds(r0 + s, 1), :]
                k_s = k_scr[pl.ds(r0 + s, 1), :]
                v_s = v_scr[pl.ds(r0 + s, 1), :]
                e = jnp.where(row_local >= s, jnp.exp(jnp.minimum(b_blk - b_s, 0.0)), 0.0)
                col = jnp.sum(q_blk * k_s * e, axis=-1, keepdims=True)
                acc = acc + col * v_s
            od_scr[pl.ds(r0, GLA_SUB), :] = acc
        return carry

    def head_post(h, slot, pre, s_t):
        b_scr, od_scr = row_scr[3].at[slot], row_scr[4].at[slot]
        qd, kd, vv, o_off = pre
        o_inter = []
        for c in range(n_chunks):
            sl = slice(c * GLA_CHUNK, (c + 1) * GLA_CHUNK)
            o_inter.append(_bdot_general(qd[sl], s_t, NT_DIMS))
            f_last = jnp.exp(b_scr[(c + 1) * GLA_CHUNK - 1:(c + 1) * GLA_CHUNK, :])
            s_t = s_t * f_last + _bdot_general(vv[sl], kd[sl], TN_DIMS)
            yield

        o = jnp.concatenate(o_inter, axis=0) + o_off + od_scr[...]
        on = o * lax.rsqrt(jnp.mean(o * o, axis=-1, keepdims=True) + EPS) * nb_ref[...]
        cols = pl.ds(pl.multiple_of(h * HEAD_DIM, HEAD_DIM), HEAD_DIM)
        z = z_ref[:, cols]
        o_ref[:, cols] = (on[:tin] * _sigmoid(z)).astype(BF16)
        return s_t

    def head_group(hg, carry):
        heads = [hg * HEAD_GROUP + u for u in range(HEAD_GROUP)]
        states = [s_scr[h] for h in heads]
        pre = _trace_round_robin([head_pre(h, u) for u, h in enumerate(heads)])
        lax.fori_loop(0, R // GLA_SUB, diag_block, 0, unroll=DIAG_UNROLL)
        states = _trace_round_robin([head_post(h, u, p, s) for u, (h, p, s) in enumerate(zip(heads, pre, states))])
        for h, s in zip(heads, states):
            s_scr[h] = s
        return carry

    lax.fori_loop(0, N_HEADS // HEAD_GROUP, head_group, 0)

    @pl.when(t == nt - 1)
    def _():
        sout_ref[0] = s_scr[...]


def _gla(proj, lb_logits, norm_b, s0_t, *, batch, tin, nt, row0, layer):
    rb0 = row0 // tin
    body = functools.partial(_gla_body, tin=tin, nt=nt, layer=layer)
    col_spec = lambda c: pl.BlockSpec((tin, D_HEADS), lambda b, t: (rb0 + b * nt + t, c))
    head_rows = pltpu.VMEM((HEAD_GROUP, BLOCK_ROWS, HEAD_DIM), F32)
    return pl.pallas_call(
        body,
        grid=(batch, nt),
        in_specs=[
            col_spec(4), col_spec(5), col_spec(6), col_spec(7),
            pl.BlockSpec(lb_logits.shape, lambda b, t: (0, 0)),
            pl.BlockSpec((1, HEAD_DIM), lambda b, t: (0, 0)),
            pl.BlockSpec((1, N_HEADS, HEAD_DIM, HEAD_DIM), lambda b, t: (b, 0, 0, 0)),
        ],
        out_specs=[
            pl.BlockSpec((tin, D_HEADS), lambda b, t: (b * nt + t, 0)),
            pl.BlockSpec((1, N_HEADS, HEAD_DIM, HEAD_DIM), lambda b, t: (b, 0, 0, 0)),
        ],
        out_shape=[jax.ShapeDtypeStruct((batch * nt * tin, D_HEADS), BF16),
                   jax.ShapeDtypeStruct((batch, N_HEADS, HEAD_DIM, HEAD_DIM), F32)],
        scratch_shapes=[pltpu.VMEM((N_HEADS, HEAD_DIM, HEAD_DIM), F32),
                        head_rows, head_rows, head_rows, head_rows, head_rows],
        compiler_params=pltpu.CompilerParams(
            dimension_semantics=("parallel", "arbitrary"), vmem_limit_bytes=VMEM_LIMIT),
        name="hgrn2",
    )(proj, proj, proj, proj, lb_logits, norm_b, s0_t)


def _merge_body(oa_ref, ob_ref, g0_ref, g1_ref, x_ref, wb0_ref, wb1_ref, wo_ref, nf_ref, wr_ref, br_ref,
                x1_ref, h_ref, lg_ref):
    ua = jnp.dot(oa_ref[...], wb0_ref[...], preferred_element_type=F32)
    ub = jnp.dot(ob_ref[...], wb1_ref[...], preferred_element_type=F32)
    merged = _sigmoid(g0_ref[...]) * ua + _sigmoid(g1_ref[...]) * ub
    x1 = x_ref[...] + jnp.dot(merged.astype(BF16), wo_ref[...], preferred_element_type=F32)
    x1_ref[...] = x1
    hn = x1 * lax.rsqrt(jnp.mean(x1 * x1, axis=-1, keepdims=True) + EPS) * nf_ref[...]
    h_ref[...] = hn
    h_hi = hn.astype(BF16)
    h_lo = (hn - h_hi.astype(F32)).astype(BF16)
    lg_ref[...] = jnp.dot(jnp.concatenate([h_hi, h_hi, h_lo], axis=1), wr_ref[...],
                          preferred_element_type=F32) + br_ref[...]


def _merge(o_a, o_b, proj, x, wb0, wb1, wo, norm_ffn, w_router, b_router, tm):
    n, d = x.shape
    const = lambda shape: pl.BlockSpec(shape, lambda i: (0, 0), pipeline_mode=pl.Buffered(1))
    return pl.pallas_call(
        _merge_body,
        grid=(n // tm,),
        in_specs=[
            pl.BlockSpec((tm, D_HEADS), lambda i: (i, 0)),
            pl.BlockSpec((tm, D_HEADS), lambda i: (i, 0)),
            pl.BlockSpec((tm, d), lambda i: (i, 8)),
            pl.BlockSpec((tm, d), lambda i: (i, 9)),
            pl.BlockSpec((tm, d), lambda i: (i, 0)),
            const(wb0.shape), const(wb1.shape), const(wo.shape),
            const((1, d)), const(w_router.shape), const((1, HEAD_DIM)),
        ],
        out_specs=[pl.BlockSpec((tm, d), lambda i: (i, 0)),
                   pl.BlockSpec((tm, d), lambda i: (i, 0)),
                   pl.BlockSpec((tm, HEAD_DIM), lambda i: (i, 0))],
        out_shape=[jax.ShapeDtypeStruct((n, d), F32),
                   jax.ShapeDtypeStruct((n, d), F32),
                   jax.ShapeDtypeStruct((n, HEAD_DIM), F32)],
        compiler_params=pltpu.CompilerParams(
            dimension_semantics=("parallel",), vmem_limit_bytes=VMEM_LIMIT),
        name="merge_out",
    )(o_a, o_b, proj, proj, x, wb0, wb1, wo, norm_ffn, w_router, b_router)


def _moe_body(be_ref, nu_ref, x_ref, wg_ref, wu_ref, bg_ref, bu_ref, wd_ref, bd_ref, o_ref):
    i = pl.program_id(0)
    f = pl.program_id(1)

    @pl.when(i < nu_ref[0])
    def _():
        x = x_ref[...].astype(BF16)
        gate = jnp.dot(x, wg_ref[0], preferred_element_type=F32) + bg_ref[0]
        up = jnp.dot(x, wu_ref[0], preferred_element_type=F32) + bu_ref[0]
        gate = jnp.minimum(gate, SWIGLU_LIMIT)
        up = jnp.clip(up, -SWIGLU_LIMIT, SWIGLU_LIMIT)
        act = (up + 1.0) * gate * _sigmoid(SWIGLU_ALPHA * gate)
        part = jnp.dot(act.astype(BF16), wd_ref[0], preferred_element_type=F32)

        @pl.when(f == 0)
        def _():
            o_ref[...] = part + bd_ref[0]

        @pl.when(f > 0)
        def _():
            o_ref[...] += part


def _moe(block_expert, n_used, xs, w_gu, b_gu, w_d, b_d):
    n_rows, d = xs.shape
    n_exp, _, two_ff = w_gu.shape
    d_ff = two_ff // 2
    tf = min(MOE_FF_TILE, d_ff)
    nf = d_ff // tf
    nblk = n_rows // MOE_ROWS
    grid_spec = pltpu.PrefetchScalarGridSpec(
        num_scalar_prefetch=2,
        grid=(nblk, nf),
        in_specs=[
            pl.BlockSpec((MOE_ROWS, d), lambda i, f, be, nu: (i, 0)),
            pl.BlockSpec((1, d, tf), lambda i, f, be, nu: (be[i], 0, f)),
            pl.BlockSpec((1, d, tf), lambda i, f, be, nu: (be[i], 0, nf + f)),
            pl.BlockSpec((1, 1, tf), lambda i, f, be, nu: (be[i], 0, f)),
            pl.BlockSpec((1, 1, tf), lambda i, f, be, nu: (be[i], 0, nf + f)),
            pl.BlockSpec((1, tf, d), lambda i, f, be, nu: (be[i], f, 0)),
            pl.BlockSpec((1, 1, d), lambda i, f, be, nu: (be[i], 0, 0)),
        ],
        out_specs=pl.BlockSpec((MOE_ROWS, d), lambda i, f, be, nu: (i, 0)),
    )
    return pl.pallas_call(
        _moe_body,
        grid_spec=grid_spec,
        out_shape=jax.ShapeDtypeStruct((n_rows, d), F32),
        compiler_params=pltpu.CompilerParams(
            dimension_semantics=("parallel", "arbitrary"), vmem_limit_bytes=VMEM_LIMIT),
        name="expert_mlp",
    )(block_expert, n_used, xs, w_gu, w_gu, b_gu, b_gu, w_d, b_d)


def _row_copy(src_ref, src_row, dst_ref, dst_row, sem):
    return pltpu.make_async_copy(src_ref.at[pl.ds(src_row, 1)], dst_ref.at[pl.ds(dst_row, 1)], sem)


def _dispatch_body(dest_ref, h_ref, xs_in_ref, xs_ref, sem, *, tm):
    del xs_in_ref

    def start(r, carry):
        for k in range(TOP_K):
            _row_copy(h_ref, r, xs_ref, dest_ref[0, 0, r * TOP_K + k], sem).start()
        return carry

    def wait(r, carry):
        for _ in range(TOP_K):
            _row_copy(h_ref, 0, xs_ref, 0, sem).wait()
        return carry

    lax.fori_loop(0, tm, start, 0, unroll=ROW_COPY_UNROLL)
    lax.fori_loop(0, tm, wait, 0, unroll=ROW_COPY_UNROLL)


def _dispatch(dest, h, n_rows, tm):
    n, d = h.shape
    xs0 = jnp.zeros((n_rows, d), F32)
    return pl.pallas_call(
        functools.partial(_dispatch_body, tm=tm),
        grid=(n // tm,),
        in_specs=[pl.BlockSpec((1, 1, tm * TOP_K), lambda i: (i, 0, 0), memory_space=pltpu.SMEM),
                  pl.BlockSpec((tm, d), lambda i: (i, 0)),
                  pl.BlockSpec(memory_space=pl.ANY)],
        out_specs=pl.BlockSpec(memory_space=pl.ANY),
        out_shape=jax.ShapeDtypeStruct((n_rows, d), F32),
        scratch_shapes=[pltpu.SemaphoreType.DMA(())],
        input_output_aliases={2: 0},
        compiler_params=pltpu.CompilerParams(
            dimension_semantics=("arbitrary",), vmem_limit_bytes=VMEM_LIMIT),
        name="dispatch_rows",
    )(dest.reshape(n // tm, 1, tm * TOP_K), h, xs0)


def _combine_body(dest_ref, gate_ref, x_ref, g_ref, ys_ref, o_ref, ybuf, sem, *, tm, apply_norm):
    def start(r, carry):
        for k in range(TOP_K):
            _row_copy(ys_ref, dest_ref[0, 0, r * TOP_K + k], ybuf.at[k], r, sem).start()
        return carry

    def wait(r, carry):
        for k in range(TOP_K):
            _row_copy(ys_ref, 0, ybuf.at[k], 0, sem).wait()
        return carry

    lax.fori_loop(0, tm, start, 0, unroll=ROW_COPY_UNROLL)
    lax.fori_loop(0, tm, wait, 0, unroll=ROW_COPY_UNROLL)

    gate = gate_ref[...]
    y = gate[:, 0:1] * ybuf[0]
    for k in range(1, TOP_K):
        y = y + gate[:, k:k + 1] * ybuf[k]
    x = x_ref[...] + y
    if apply_norm:
        x = x * lax.rsqrt(jnp.mean(x * x, axis=-1, keepdims=True) + EPS) * g_ref[...]
    o_ref[...] = x


def _combine(dest, gate, x1, gain, ys, tm, *, apply_norm):
    n, d = x1.shape
    return pl.pallas_call(
        functools.partial(_combine_body, tm=tm, apply_norm=apply_norm),
        grid=(n // tm,),
        in_specs=[pl.BlockSpec((1, 1, tm * TOP_K), lambda i: (i, 0, 0), memory_space=pltpu.SMEM),
                  pl.BlockSpec((tm, TOP_K), lambda i: (i, 0)),
                  pl.BlockSpec((tm, d), lambda i: (i, 0)),
                  pl.BlockSpec((1, d), lambda i: (0, 0)),
                  pl.BlockSpec(memory_space=pl.ANY)],
        out_specs=pl.BlockSpec((tm, d), lambda i: (i, 0)),
        out_shape=jax.ShapeDtypeStruct((n, d), F32),
        scratch_shapes=[pltpu.VMEM((TOP_K, tm, d), F32), pltpu.SemaphoreType.DMA(())],
        compiler_params=pltpu.CompilerParams(
            dimension_semantics=("arbitrary",), vmem_limit_bytes=VMEM_LIMIT),
        name="combine_final_norm",
    )(dest.reshape(n // tm, 1, tm * TOP_K), gate, x1, gain, ys)


def _token_tile(n, candidates):
    for c in candidates:
        if n % c == 0:
            return c
    raise ValueError(f"token count {n} has no tile in {candidates}")


def _lane_vector(values, first_lane):
    out = jnp.zeros((1, HEAD_DIM), F32)
    return lax.dynamic_update_slice(out, values.reshape(1, -1).astype(F32), (0, first_lane))


def _route(logits, n_exp):
    n_tok = logits.shape[0]
    top_logit, top_idx = lax.top_k(logits, TOP_K)
    gate = jax.nn.softmax(top_logit, axis=-1)
    n_assign = n_tok * TOP_K
    e_flat = top_idx.reshape(-1)
    onehot = (e_flat[:, None] == jnp.arange(n_exp)[None, :]).astype(jnp.int32)
    seen = jnp.cumsum(onehot, axis=0)
    rank = jnp.sum(onehot * seen, axis=1) - 1
    counts = seen[-1]
    padded = ((counts + MOE_ROWS - 1) // MOE_ROWS) * MOE_ROWS
    pend = jnp.cumsum(padded)
    pstart = pend - padded
    dest = (pstart[e_flat] + rank).astype(jnp.int32)
    n_blocks = -(-n_assign // MOE_ROWS) + n_exp
    block_expert = jnp.minimum(
        jnp.searchsorted(pend, jnp.arange(n_blocks) * MOE_ROWS, side='right'), n_exp - 1).astype(jnp.int32)
    n_used = (pend[-1] // MOE_ROWS).astype(jnp.int32).reshape(1)
    return gate, dest, block_expert, n_used, n_blocks * MOE_ROWS


def kernel(x_prompt, x_sample, state_conv_a, state_delta, state_hgrn, norm_mix, w_in, conv_a, a_log, dt_bias,
           norm_a, lb_logits, norm_b, w_branch, w_out, norm_ffn, w_router, b_router, w_gate_up, b_gate_up,
           w_down, b_down, norm_final):
    bp, tp, d = x_prompt.shape
    bs, ts, _ = x_sample.shape
    depth = w_in.shape[0]
    n_exp = w_router.shape[-1]
    n_p, n_s = bp * tp, bs * ts
    n = n_p + n_s
    assert d == D_HEADS and tp % BLOCK_ROWS == 0 and ts <= BLOCK_ROWS and n_p % ts == 0
    assert min(tp, ts) >= CONV_WIDTH - 1
    d_conv = 3 * D_HEADS
    tm_big = _token_tile(n, (768, 512, 256))
    tm_small = _token_tile(n, (256,))

    x = jnp.concatenate([x_prompt.reshape(n_p, d), x_sample.reshape(n_s, d)], axis=0)
    conv_p, delta_p, hgrn_p, conv_s, delta_s, hgrn_s = [], [], [], [], [], []
    for l in range(depth):
        w_l = w_in[l]
        c0 = d_conv
        w_main = jnp.concatenate([w_l[:, :c0], w_l[:, c0 + 2 * N_HEADS:]], axis=1).astype(BF16)
        w_ba = jnp.pad(w_l[:, c0:c0 + 2 * N_HEADS], ((0, 0), (0, HEAD_DIM - 2 * N_HEADS))).astype(BF16)
        gain = norm_mix[l].reshape(1, d)
        proj = _inproj(x, gain, w_main, tm_big, D_HEADS)
        ba = _inproj(x, gain, w_ba, tm_big, HEAD_DIM)

        conv_w = jnp.pad(conv_a[l], ((0, CONV_PAD - CONV_WIDTH), (0, 0)))
        alog_l = _lane_vector(a_log[l], N_HEADS)
        dtb_l = _lane_vector(dt_bias[l], N_HEADS)
        na = norm_a[l].reshape(1, HEAD_DIM)
        nb = norm_b[l].reshape(1, HEAD_DIM)
        lbl = jnp.pad(lb_logits.astype(F32), ((0, CONV_PAD - lb_logits.shape[0]), (0, 0)), constant_values=-1e30)

        zero_cs = jnp.zeros((bp, CONV_PAD, d_conv), F32)
        zero_s = jnp.zeros((bp, N_HEADS, HEAD_DIM, HEAD_DIM), F32)
        cs_s = jnp.pad(state_conv_a[l], ((0, 0), (CONV_PAD - (CONV_WIDTH - 1), 0), (0, 0)))

        oa_p, sd_p = _gdn(proj, ba, zero_cs, zero_s, conv_w, alog_l, dtb_l, na,
                          batch=bp, tin=BLOCK_ROWS, nt=tp // BLOCK_ROWS, row0=0)
        oa_s, sd_s = _gdn(proj, ba, cs_s, state_delta[l], conv_w, alog_l, dtb_l, na,
                          batch=bs, tin=ts, nt=1, row0=n_p)
        ob_p, sh_p = _gla(proj, lbl, nb, zero_s, batch=bp, tin=BLOCK_ROWS, nt=tp // BLOCK_ROWS, row0=0, layer=l)
        ob_s, sh_s = _gla(proj, lbl, nb, jnp.swapaxes(state_hgrn[l], -1, -2),
                          batch=bs, tin=ts, nt=1, row0=n_p, layer=l)
        o_a = jnp.concatenate([oa_p, oa_s], axis=0)
        o_b = jnp.concatenate([ob_p, ob_s], axis=0)

        tail = CONV_WIDTH - 1
        conv_p.append(jnp.stack([proj[(b + 1) * tp - tail:(b + 1) * tp, :d_conv] for b in range(bp)]))
        conv_s.append(jnp.stack([proj[n_p + (b + 1) * ts - tail:n_p + (b + 1) * ts, :d_conv] for b in range(bs)]))
        delta_p.append(sd_p)
        delta_s.append(sd_s)
        hgrn_p.append(jnp.swapaxes(sh_p, -1, -2))
        hgrn_s.append(jnp.swapaxes(sh_s, -1, -2))

        w_r = jnp.pad(w_router[l], ((0, 0), (0, HEAD_DIM - n_exp)))
        w_r_hi = w_r.astype(BF16)
        w_r = jnp.concatenate([w_r_hi, (w_r - w_r_hi.astype(F32)).astype(BF16), w_r_hi], axis=0)
        b_r = jnp.pad(b_router[l], (0, HEAD_DIM - n_exp)).reshape(1, HEAD_DIM)
        x1, h, logits = _merge(o_a, o_b, proj, x, w_branch[l, 0].astype(BF16), w_branch[l, 1].astype(BF16),
                               w_out[l].astype(BF16), norm_ffn[l].reshape(1, d), w_r, b_r, tm_small)

        gate, dest, block_expert, n_used, n_rows = _route(logits[:, :n_exp], n_exp)
        xs = _dispatch(dest, h, n_rows, tm_small)
        ys = _moe(block_expert, n_used, xs, w_gate_up[l].astype(BF16), b_gate_up[l][:, None, :],
                  w_down[l].astype(BF16), b_down[l][:, None, :])
        last = l + 1 == depth
        x = _combine(dest, gate, x1, norm_final.reshape(1, d), ys, tm_small, apply_norm=last)
    y_prompt = x[:n_p].reshape(bp, tp, d)
    y_sample = x[n_p:].reshape(bs, ts, d)
    return (y_prompt, y_sample, jnp.stack(conv_p), jnp.stack(delta_p), jnp.stack(hgrn_p),
            jnp.stack(conv_s), jnp.stack(delta_s), jnp.stack(hgrn_s))
```

```python
import functools

import jax
import jax.numpy as jnp
from jax import lax
from jax.experimental import pallas as pl
from jax.experimental.pallas import tpu as pltpu

F32 = jnp.float32
BF16 = jnp.bfloat16

EPS = 1e-6
HEAD_DIM = 128
N_HEADS = 16
D_HEADS = N_HEADS * HEAD_DIM
CONV_WIDTH = 4
CONV_PAD = 8
BLOCK_ROWS = 256
GLA_CHUNK = 64
GLA_SUB = 8
HEAD_GROUP = 2
GDN_HEAD_GROUP = 4
DIAG_UNROLL = True
TOP_K = 4
SWIGLU_LIMIT = 7.0
SWIGLU_ALPHA = 1.702
MOE_ROWS = 512
MOE_FF_TILE = 1024
ROW_COPY_UNROLL = 2
VMEM_LIMIT = 56 * 1024 * 1024

NT_DIMS = (((1,), (1,)), ((), ()))
TN_DIMS = (((0,), (0,)), ((), ()))


def _sigmoid(x):
    return 1.0 / (1.0 + jnp.exp(-x))


def _bdot(a, b):
    return jnp.dot(a.astype(BF16), b.astype(BF16), preferred_element_type=F32)


def _bdot_general(a, b, dims):
    return lax.dot_general(a.astype(BF16), b.astype(BF16), dims, preferred_element_type=F32)


def _split3(x):
    hi = x.astype(BF16)
    r1 = x - hi.astype(F32)
    mid = r1.astype(BF16)
    lo = (r1 - mid.astype(F32)).astype(BF16)
    return hi, mid, lo


def _dot_x01(x, m01):
    m = m01.astype(BF16)
    return jnp.dot(jnp.concatenate(_split3(x), axis=1), jnp.concatenate([m, m, m], axis=0),
                   preferred_element_type=F32)


def _dot_01x(m01, x):
    m = m01.astype(BF16)
    return jnp.dot(jnp.concatenate([m, m, m], axis=1), jnp.concatenate(_split3(x), axis=0),
                   preferred_element_type=F32)


def _dot_01xt(m01, x):
    m = m01.astype(BF16)
    return lax.dot_general(jnp.concatenate([m, m, m], axis=1), jnp.concatenate(_split3(x), axis=1), NT_DIMS,
                           preferred_element_type=F32)


def _trace_round_robin(gens):
    results = [None] * len(gens)
    live = list(range(len(gens)))
    while live:
        for i in list(live):
            try:
                next(gens[i])
            except StopIteration as stop:
                results[i] = stop.value
                live.remove(i)
    return results


def _inproj_body(x_ref, g_ref, w_ref, o_ref, xn_ref):
    @pl.when(pl.program_id(1) == 0)
    def _():
        x = x_ref[...]
        ms = jnp.mean(x * x, axis=-1, keepdims=True)
        xn_ref[...] = (x * lax.rsqrt(ms + EPS) * g_ref[...]).astype(BF16)

    o_ref[...] = jnp.dot(xn_ref[...], w_ref[...], preferred_element_type=F32)


def _inproj(x, gain, w, tm, tn):
    n, d = x.shape
    nc = w.shape[1]
    return pl.pallas_call(
        _inproj_body,
        grid=(n // tm, nc // tn),
        in_specs=[pl.BlockSpec((tm, d), lambda i, j: (i, 0)),
                  pl.BlockSpec((1, d), lambda i, j: (0, 0)),
                  pl.BlockSpec((d, tn), lambda i, j: (0, j))],
        out_specs=pl.BlockSpec((tm, tn), lambda i, j: (i, j)),
        out_shape=jax.ShapeDtypeStruct((n, nc), F32),
        scratch_shapes=[pltpu.VMEM((tm, d), BF16)],
        compiler_params=pltpu.CompilerParams(
            dimension_semantics=("parallel", "arbitrary"), vmem_limit_bytes=VMEM_LIMIT),
        name="inproj",
    )(x, gain, w)


def _pad_rows(x, rows):
    if x.shape[0] == rows:
        return x
    return jnp.concatenate([x, jnp.zeros((rows - x.shape[0],) + x.shape[1:], x.dtype)], axis=0)


def _gdn_body(qkv_ref, z_ref, ba_ref, cs_ref, s0_ref, cw_ref, alog_ref, dtb_ref, na_ref,
              o_ref, sout_ref, xbuf, s_scr, *, tin, nt):
    R = BLOCK_ROWS
    t = pl.program_id(1)

    @pl.when(t == 0)
    def _():
        xbuf[0:CONV_PAD, :] = cs_ref[0]
        s_scr[...] = s0_ref[0]

    if nt > 1:
        @pl.when(t > 0)
        def _():
            xbuf[0:CONV_PAD, :] = xbuf[tin:tin + CONV_PAD, :]

    xbuf[CONV_PAD:CONV_PAD + tin, :] = qkv_ref[...]
    if tin < R:
        xbuf[CONV_PAD + tin:CONV_PAD + R, :] = jnp.zeros((R - tin, xbuf.shape[1]), F32)

    rows = lax.broadcasted_iota(jnp.int32, (R, HEAD_DIM), 0)
    lanes = lax.broadcasted_iota(jnp.int32, (R, HEAD_DIM), 1)
    valid = rows < tin
    ti = lax.broadcasted_iota(jnp.int32, (R, R), 0)
    si = lax.broadcasted_iota(jnp.int32, (R, R), 1)
    causal = ti >= si
    merge_key = jnp.where(ti > si, ti ^ si, 0)

    ba = _pad_rows(ba_ref[...], R)
    sp_in = ba + dtb_ref[...]
    softplus = jnp.maximum(sp_in, 0.0) + jnp.log(1.0 + jnp.exp(-jnp.abs(sp_in)))
    g_all = jnp.where(valid, -jnp.exp(alog_ref[...]) * softplus, 0.0)
    beta_all = jnp.where(valid, _sigmoid(ba), 0.0)
    gcum_all = _dot_01x(causal, g_all)
    gate_src = jnp.where(lanes < N_HEADS, beta_all, gcum_all)
    HALF = R // 2
    half_key = merge_key[:HALF, :HALF]

    sel_l = lax.broadcasted_iota(jnp.int32, (HEAD_DIM, 2 * HEAD_DIM), 0)
    sel_c = lax.broadcasted_iota(jnp.int32, (HEAD_DIM, 2 * HEAD_DIM), 1)
    lane0 = (lanes == 0).astype(F32)

    def head(h, s_old):
        off = h * HEAD_DIM

        def conv_silu(base):
            cols = pl.ds(pl.multiple_of(base + off, HEAD_DIM), HEAD_DIM)
            acc = None
            for j in range(CONV_WIDTH):
                term = xbuf[pl.ds(CONV_PAD - (CONV_WIDTH - 1) + j, R), cols] * cw_ref[j:j + 1, cols]
                acc = term if acc is None else acc + term
            return acc * _sigmoid(acc)

        qc = conv_silu(0)
        kc = conv_silu(D_HEADS)
        vc = conv_silu(2 * D_HEADS)
        qn = qc * lax.rsqrt(jnp.sum(qc * qc, axis=-1, keepdims=True) + EPS) * (HEAD_DIM ** -0.5)
        kn = jnp.where(valid, kc * lax.rsqrt(jnp.sum(kc * kc, axis=-1, keepdims=True) + EPS), 0.0)

        sel = (((sel_c < HEAD_DIM) & (sel_l == h)) |
               ((sel_c >= HEAD_DIM) & (sel_l == h + N_HEADS))).astype(F32)
        bg = _dot_x01(gate_src, sel)
        yield
        beta = bg[:, :HEAD_DIM]
        gc = bg[:, HEAD_DIM:]
        gc_last = gc[R - 1:R, :]
        gc_t = jnp.concatenate([gc, gc], axis=1)
        gc_s = _dot_01xt(lane0, gc)
        yield
        decay = jnp.where(causal, jnp.exp(jnp.where(causal, gc_t - gc_s, 0.0)), 0.0)

        kb = kn * beta
        aq = _bdot_general(jnp.concatenate([kb, qn], axis=0), kn, NT_DIMS)
        yield
        a_mat = jnp.where(ti > si, aq[:R] * decay, 0.0)
        qk = aq[R:] * decay

        a_top, a_bot, a_cross = a_mat[:HALF, :HALF], a_mat[HALF:, HALF:], a_mat[HALF:, :HALF]
        n_top = n_bot = None
        b = 1
        while b < HALF:
            level = (half_key >= b) & (half_key < 2 * b)
            c_top = jnp.where(level, a_top, 0.0)
            c_bot = jnp.where(level, a_bot, 0.0)
            if n_top is None:
                n_top, n_bot = -c_top, -c_bot
            else:
                p_top = c_top + _bdot(n_top, c_top)
                p_bot = c_bot + _bdot(n_bot, c_bot)
                yield
                n_top = n_top - p_top - _bdot(p_top, n_top)
                n_bot = n_bot - p_bot - _bdot(p_bot, n_bot)
                yield
            b *= 2
        p_cross = a_cross + _bdot(n_bot, a_cross)
        yield
        n_cross = -(p_cross + _bdot(p_cross, n_top))
        yield

        rhs = jnp.concatenate([vc * beta, kb * jnp.exp(gc)], axis=1)
        rhs_top, rhs_bot = rhs[:HALF], rhs[HALF:]
        sol = jnp.concatenate([rhs_top + _bdot(n_top, rhs_top),
                               rhs_bot + _bdot(n_cross, rhs_top) + _bdot(n_bot, rhs_bot)], axis=0)
        yield
        u = sol[:, :HEAD_DIM]
        w = sol[:, HEAD_DIM:]

        qd = qn * jnp.exp(gc)
        kd = kn * jnp.exp(gc_last - gc)
        ws = _bdot(jnp.concatenate([w, qd], axis=0), s_old)
        yield
        v_new = u - ws[:R]
        o = ws[R:] + _bdot(qk, v_new)
        s_new = s_old * jnp.exp(gc_last) + _bdot_general(kd, v_new, TN_DIMS)
        yield

        cols = pl.ds(pl.multiple_of(off, HEAD_DIM), HEAD_DIM)
        on = o * lax.rsqrt(jnp.mean(o * o, axis=-1, keepdims=True) + EPS) * na_ref[...]
        z = z_ref[:, cols]
        o_ref[:, cols] = (on[:tin] * (z * _sigmoid(z))).astype(BF16)
        return s_new

    def head_group(hg, carry):
        heads = [hg * GDN_HEAD_GROUP + u for u in range(GDN_HEAD_GROUP)]
        states = _trace_round_robin([head(h, s_scr[h]) for h in heads])
        for h, s in zip(heads, states):
            s_scr[h] = s
        return carry

    lax.fori_loop(0, N_HEADS // GDN_HEAD_GROUP, head_group, 0)

    @pl.when(t == nt - 1)
    def _():
        sout_ref[0] = s_scr[...]


def _gdn(proj, ba, conv_state, s0, conv_w, alog_l, dtb_l, norm_a, *, batch, tin, nt):
    body = functools.partial(_gdn_body, tin=tin, nt=nt)
    return pl.pallas_call(
        body,
        grid=(batch, nt),
        in_specs=[
            pl.BlockSpec((tin, 3 * D_HEADS), lambda b, t: (b * nt + t, 0)),
            pl.BlockSpec((tin, D_HEADS), lambda b, t: (b * nt + t, 3)),
            pl.BlockSpec((tin, HEAD_DIM), lambda b, t: (b * nt + t, 0)),
            pl.BlockSpec((1, CONV_PAD, 3 * D_HEADS), lambda b, t: (b, 0, 0)),
            pl.BlockSpec((1, N_HEADS, HEAD_DIM, HEAD_DIM), lambda b, t: (b, 0, 0, 0)),
            pl.BlockSpec((CONV_PAD, 3 * D_HEADS), lambda b, t: (0, 0)),
            pl.BlockSpec((1, HEAD_DIM), lambda b, t: (0, 0)),
            pl.BlockSpec((1, HEAD_DIM), lambda b, t: (0, 0)),
            pl.BlockSpec((1, HEAD_DIM), lambda b, t: (0, 0)),
        ],
        out_specs=[
            pl.BlockSpec((tin, D_HEADS), lambda b, t: (b * nt + t, 0)),
            pl.BlockSpec((1, N_HEADS, HEAD_DIM, HEAD_DIM), lambda b, t: (b, 0, 0, 0)),
        ],
        out_shape=[jax.ShapeDtypeStruct((batch * nt * tin, D_HEADS), BF16),
                   jax.ShapeDtypeStruct((batch, N_HEADS, HEAD_DIM, HEAD_DIM), F32)],
        scratch_shapes=[pltpu.VMEM((BLOCK_ROWS + CONV_PAD, 3 * D_HEADS), F32),
                        pltpu.VMEM((N_HEADS, HEAD_DIM, HEAD_DIM), F32)],
        compiler_params=pltpu.CompilerParams(
            dimension_semantics=("parallel", "arbitrary"), vmem_limit_bytes=VMEM_LIMIT),
        name="gated_delta",
    )(proj, proj, ba, conv_state, s0, conv_w, alog_l, dtb_l, norm_a)


def _gla_body(q_ref, f_ref, i_ref, z_ref, lbl_ref, nb_ref, s0_ref, o_ref, sout_ref,
              s_scr, *row_scr, tin, nt, layer):
    R = BLOCK_ROWS
    n_chunks = R // GLA_CHUNK
    t = pl.program_id(1)

    @pl.when(t == 0)
    def _():
        s_scr[...] = s0_ref[0]

    rows = lax.broadcasted_iota(jnp.int32, (R, HEAD_DIM), 0)
    valid = rows < tin
    ti = lax.broadcasted_iota(jnp.int32, (R, R), 0)
    si = lax.broadcasted_iota(jnp.int32, (R, R), 1)
    chunk_tri = ((ti >= si) & ((ti // GLA_CHUNK) == (si // GLA_CHUNK))).astype(F32)
    pair_key = jnp.where(ti > si, ti ^ si, 0)
    row_local = lax.broadcasted_iota(jnp.int32, (GLA_SUB, HEAD_DIM), 0)

    def head_pre(h, slot):
        q_scr, k_scr, v_scr, b_scr, _ = (r.at[slot] for r in row_scr)
        cols = pl.ds(pl.multiple_of(h * HEAD_DIM, HEAD_DIM), HEAD_DIM)
        lbl = lbl_ref[:, cols]
        lb_e = jnp.exp(lbl - jnp.max(lbl, axis=0, keepdims=True))
        lb = jnp.sum(lb_e[0:layer + 1], axis=0, keepdims=True) / jnp.sum(lb_e, axis=0, keepdims=True)

        fl = _pad_rows(f_ref[:, cols], R)
        qx = _pad_rows(q_ref[:, cols], R)
        vv = _pad_rows(i_ref[:, cols], R)
        log_f = jnp.where(valid, jnp.log(lb + (1.0 - lb) * _sigmoid(fl)), 0.0)
        kk = jnp.where(valid, (1.0 - lb) * _sigmoid(-fl), 0.0)
        qq = qx * _sigmoid(qx)
        bc = _dot_01x(chunk_tri, log_f)
        yield
        q_scr[...] = qq
        k_scr[...] = kk
        v_scr[...] = vv
        b_scr[...] = bc

        def block_row(period, row):
            return jnp.concatenate(
                [jnp.broadcast_to(b_scr[start + row:start + row + 1, :], (period, HEAD_DIM))
                 for start in range(0, R, period)], axis=0)

        b_last = block_row(GLA_CHUNK, GLA_CHUNK - 1)
        qd = qq * jnp.exp(bc)
        kd = kk * jnp.exp(b_last - bc)

        scores = jnp.zeros((R, R), F32)
        hs = GLA_SUB
        while hs < GLA_CHUNK:
            b_mid = block_row(2 * hs, hs - 1)
            lower = (rows & hs) != 0
            q_fac = jnp.where(lower, qq * jnp.exp(jnp.minimum(bc - b_mid, 0.0)), 0.0)
            k_fac = jnp.where(lower, 0.0, kk * jnp.exp(jnp.minimum(b_mid - bc, 0.0)))
            level = _bdot_general(q_fac, k_fac, NT_DIMS)
            yield
            scores = jnp.where((pair_key >= hs) & (pair_key < 2 * hs), level, scores)
            hs *= 2
        o_off = _bdot(scores, vv)
        yield
        return qd, kd, vv, o_off

    def diag_block(m, carry):
        r0 = pl.multiple_of(m * GLA_SUB, GLA_SUB)
        for slot in range(HEAD_GROUP):
            q_scr, k_scr, v_scr, b_scr, od_scr = (r.at[slot] for r in row_scr)
            q_blk = q_scr[pl.ds(r0, GLA_SUB), :]
            b_blk = b_scr[pl.ds(r0, GLA_SUB), :]
            acc = jnp.zeros((GLA_SUB, HEAD_DIM), F32)
            for s in range(GLA_SUB):
                b_s = b_scr[pl.ds(r0 + s, 1), :]
                k_s = k_scr[pl.ds(r0 + s, 1), :]
                v_s = v_scr[pl.ds(r0 + s, 1), :]
                e = jnp.where(row_local >= s, jnp.exp(jnp.minimum(b_blk - b_s, 0.0)), 0.0)
                col = jnp.sum(q_blk * k_s * e, axis=-1, keepdims=True)
                acc = acc + col * v_s
            od_scr[pl.ds(r0, GLA_SUB), :] = acc
        return carry

    def head_post(h, slot, pre, s_t):
        b_scr, od_scr = row_scr[3].at[slot], row_scr[4].at[slot]
        qd, kd, vv, o_off = pre
        o_inter = []
        for c in range(n_chunks):
            sl = slice(c * GLA_CHUNK, (c + 1) * GLA_CHUNK)
            o_inter.append(_bdot_general(qd[sl], s_t, NT_DIMS))
            f_last = jnp.exp(b_scr[(c + 1) * GLA_CHUNK - 1:(c + 1) * GLA_CHUNK, :])
            s_t = s_t * f_last + _bdot_general(vv[sl], kd[sl], TN_DIMS)
            yield

        o = jnp.concatenate(o_inter, axis=0) + o_off + od_scr[...]
        on = o * lax.rsqrt(jnp.mean(o * o, axis=-1, keepdims=True) + EPS) * nb_ref[...]
        cols = pl.ds(pl.multiple_of(h * HEAD_DIM, HEAD_DIM), HEAD_DIM)
        z = z_ref[:, cols]
        o_ref[:, cols] = (on[:tin] * _sigmoid(z)).astype(BF16)
        return s_t

    def head_group(hg, carry):
        heads = [hg * HEAD_GROUP + u for u in range(HEAD_GROUP)]
        states = [s_scr[h] for h in heads]
        pre = _trace_round_robin([head_pre(h, u) for u, h in enumerate(heads)])
        lax.fori_loop(0, R // GLA_SUB, diag_block, 0, unroll=DIAG_UNROLL)
        states = _trace_round_robin([head_post(h, u, p, s) for u, (h, p, s) in enumerate(zip(heads, pre, states))])
        for h, s in zip(heads, states):
            s_scr[h] = s
        return carry

    lax.fori_loop(0, N_HEADS // HEAD_GROUP, head_group, 0)

    @pl.when(t == nt - 1)
    def _():
        sout_ref[0] = s_scr[...]


def _gla(proj, lb_logits, norm_b, s0_t, *, batch, tin, nt, layer):
    body = functools.partial(_gla_body, tin=tin, nt=nt, layer=layer)
    col_spec = lambda c: pl.BlockSpec((tin, D_HEADS), lambda b, t: (b * nt + t, c))
    head_rows = pltpu.VMEM((HEAD_GROUP, BLOCK_ROWS, HEAD_DIM), F32)
    return pl.pallas_call(
        body,
        grid=(batch, nt),
        in_specs=[
            col_spec(4), col_spec(5), col_spec(6), col_spec(7),
            pl.BlockSpec(lb_logits.shape, lambda b, t: (0, 0)),
            pl.BlockSpec((1, HEAD_DIM), lambda b, t: (0, 0)),
            pl.BlockSpec((1, N_HEADS, HEAD_DIM, HEAD_DIM), lambda b, t: (b, 0, 0, 0)),
        ],
        out_specs=[
            pl.BlockSpec((tin, D_HEADS), lambda b, t: (b * nt + t, 0)),
            pl.BlockSpec((1, N_HEADS, HEAD_DIM, HEAD_DIM), lambda b, t: (b, 0, 0, 0)),
        ],
        out_shape=[jax.ShapeDtypeStruct((batch * nt * tin, D_HEADS), BF16),
                   jax.ShapeDtypeStruct((batch, N_HEADS, HEAD_DIM, HEAD_DIM), F32)],
        scratch_shapes=[pltpu.VMEM((N_HEADS, HEAD_DIM, HEAD_DIM), F32),
                        head_rows, head_rows, head_rows, head_rows, head_rows],
        compiler_params=pltpu.CompilerParams(
            dimension_semantics=("parallel", "arbitrary"), vmem_limit_bytes=VMEM_LIMIT),
        name="hgrn2",
    )(proj, proj, proj, proj, lb_logits, norm_b, s0_t)


def _merge_body(oa_ref, ob_ref, g0_ref, g1_ref, x_ref, wb0_ref, wb1_ref, wo_ref, nf_ref, wr_ref, br_ref,
                x1_ref, h_ref, lg_ref):
    ua = jnp.dot(oa_ref[...], wb0_ref[...], preferred_element_type=F32)
    ub = jnp.dot(ob_ref[...], wb1_ref[...], preferred_element_type=F32)
    merged = _sigmoid(g0_ref[...]) * ua + _sigmoid(g1_ref[...]) * ub
    x1 = x_ref[...] + jnp.dot(merged.astype(BF16), wo_ref[...], preferred_element_type=F32)
    x1_ref[...] = x1
    hn = x1 * lax.rsqrt(jnp.mean(x1 * x1, axis=-1, keepdims=True) + EPS) * nf_ref[...]
    h_ref[...] = hn
    h_hi = hn.astype(BF16)
    h_lo = (hn - h_hi.astype(F32)).astype(BF16)
    lg_ref[...] = jnp.dot(jnp.concatenate([h_hi, h_hi, h_lo], axis=1), wr_ref[...],
                          preferred_element_type=F32) + br_ref[...]


def _merge(o_a, o_b, proj, x, wb0, wb1, wo, norm_ffn, w_router, b_router, tm):
    n, d = x.shape
    const = lambda shape: pl.BlockSpec(shape, lambda i: (0, 0), pipeline_mode=pl.Buffered(1))
    return pl.pallas_call(
        _merge_body,
        grid=(n // tm,),
        in_specs=[
            pl.BlockSpec((tm, D_HEADS), lambda i: (i, 0)),
            pl.BlockSpec((tm, D_HEADS), lambda i: (i, 0)),
            pl.BlockSpec((tm, d), lambda i: (i, 8)),
            pl.BlockSpec((tm, d), lambda i: (i, 9)),
            pl.BlockSpec((tm, d), lambda i: (i, 0)),
            const(wb0.shape), const(wb1.shape), const(wo.shape),
            const((1, d)), const(w_router.shape), const((1, HEAD_DIM)),
        ],
        out_specs=[pl.BlockSpec((tm, d), lambda i: (i, 0)),
                   pl.BlockSpec((tm, d), lambda i: (i, 0)),
                   pl.BlockSpec((tm, HEAD_DIM), lambda i: (i, 0))],
        out_shape=[jax.ShapeDtypeStruct((n, d), F32),
                   jax.ShapeDtypeStruct((n, d), F32),
                   jax.ShapeDtypeStruct((n, HEAD_DIM), F32)],
        compiler_params=pltpu.CompilerParams(
            dimension_semantics=("parallel",), vmem_limit_bytes=VMEM_LIMIT),
        name="merge_out",
    )(o_a, o_b, proj, proj, x, wb0, wb1, wo, norm_ffn, w_router, b_router)


def _moe_body(be_ref, nu_ref, x_ref, wg_ref, wu_ref, bg_ref, bu_ref, wd_ref, bd_ref, o_ref):
    i = pl.program_id(0)
    f = pl.program_id(1)

    @pl.when(i < nu_ref[0])
    def _():
        x = x_ref[...].astype(BF16)
        gate = jnp.dot(x, wg_ref[0], preferred_element_type=F32) + bg_ref[0]
        up = jnp.dot(x, wu_ref[0], preferred_element_type=F32) + bu_ref[0]
        gate = jnp.minimum(gate, SWIGLU_LIMIT)
        up = jnp.clip(up, -SWIGLU_LIMIT, SWIGLU_LIMIT)
        act = (up + 1.0) * gate * _sigmoid(SWIGLU_ALPHA * gate)
        part = jnp.dot(act.astype(BF16), wd_ref[0], preferred_element_type=F32)

        @pl.when(f == 0)
        def _():
            o_ref[...] = part + bd_ref[0]

        @pl.when(f > 0)
        def _():
            o_ref[...] += part


def _moe(block_expert, n_used, xs, w_gu, b_gu, w_d, b_d):
    n_rows, d = xs.shape
    n_exp, _, two_ff = w_gu.shape
    d_ff = two_ff // 2
    tf = min(MOE_FF_TILE, d_ff)
    nf = d_ff // tf
    nblk = n_rows // MOE_ROWS
    grid_spec = pltpu.PrefetchScalarGridSpec(
        num_scalar_prefetch=2,
        grid=(nblk, nf),
        in_specs=[
            pl.BlockSpec((MOE_ROWS, d), lambda i, f, be, nu: (i, 0)),
            pl.BlockSpec((1, d, tf), lambda i, f, be, nu: (be[i], 0, f)),
            pl.BlockSpec((1, d, tf), lambda i, f, be, nu: (be[i], 0, nf + f)),
            pl.BlockSpec((1, 1, tf), lambda i, f, be, nu: (be[i], 0, f)),
            pl.BlockSpec((1, 1, tf), lambda i, f, be, nu: (be[i], 0, nf + f)),
            pl.BlockSpec((1, tf, d), lambda i, f, be, nu: (be[i], f, 0)),
            pl.BlockSpec((1, 1, d), lambda i, f, be, nu: (be[i], 0, 0)),
        ],
        out_specs=pl.BlockSpec((MOE_ROWS, d), lambda i, f, be, nu: (i, 0)),
    )
    return pl.pallas_call(
        _moe_body,
        grid_spec=grid_spec,
        out_shape=jax.ShapeDtypeStruct((n_rows, d), F32),
        compiler_params=pltpu.CompilerParams(
            dimension_semantics=("parallel", "arbitrary"), vmem_limit_bytes=VMEM_LIMIT),
        name="expert_mlp",
    )(block_expert, n_used, xs, w_gu, w_gu, b_gu, b_gu, w_d, b_d)


def _row_copy(src_ref, src_row, dst_ref, dst_row, sem):
    return pltpu.make_async_copy(src_ref.at[pl.ds(src_row, 1)], dst_ref.at[pl.ds(dst_row, 1)], sem)


def _dispatch_body(dest_ref, h_ref, xs_in_ref, xs_ref, sem, *, tm):
    del xs_in_ref

    def start(r, carry):
        for k in range(TOP_K):
            _row_copy(h_ref, r, xs_ref, dest_ref[0, 0, r * TOP_K + k], sem).start()
        return carry

    def wait(r, carry):
        for _ in range(TOP_K):
            _row_copy(h_ref, 0, xs_ref, 0, sem).wait()
        return carry

    lax.fori_loop(0, tm, start, 0, unroll=ROW_COPY_UNROLL)
    lax.fori_loop(0, tm, wait, 0, unroll=ROW_COPY_UNROLL)


def _dispatch(dest, h, xs, tm):
    n, d = h.shape
    n_rows = xs.shape[0]
    return pl.pallas_call(
        functools.partial(_dispatch_body, tm=tm),
        grid=(n // tm,),
        in_specs=[pl.BlockSpec((1, 1, tm * TOP_K), lambda i: (i, 0, 0), memory_space=pltpu.SMEM),
                  pl.BlockSpec((tm, d), lambda i: (i, 0)),
                  pl.BlockSpec(memory_space=pl.ANY)],
        out_specs=pl.BlockSpec(memory_space=pl.ANY),
        out_shape=jax.ShapeDtypeStruct((n_rows, d), F32),
        scratch_shapes=[pltpu.SemaphoreType.DMA(())],
        input_output_aliases={2: 0},
        compiler_params=pltpu.CompilerParams(
            dimension_semantics=("arbitrary",), vmem_limit_bytes=VMEM_LIMIT),
        name="dispatch_rows",
    )(dest.reshape(n // tm, 1, tm * TOP_K), h, xs)


def _combine_body(dest_ref, gate_ref, x_ref, g_ref, ys_ref, o_ref, ybuf, sem, *, tm, apply_norm):
    def start(r, carry):
        for k in range(TOP_K):
            _row_copy(ys_ref, dest_ref[0, 0, r * TOP_K + k], ybuf.at[k], r, sem).start()
        return carry

    def wait(r, carry):
        for k in range(TOP_K):
            _row_copy(ys_ref, 0, ybuf.at[k], 0, sem).wait()
        return carry

    lax.fori_loop(0, tm, start, 0, unroll=ROW_COPY_UNROLL)
    lax.fori_loop(0, tm, wait, 0, unroll=ROW_COPY_UNROLL)

    gate = gate_ref[...]
    y = gate[:, 0:1] * ybuf[0]
    for k in range(1, TOP_K):
        y = y + gate[:, k:k + 1] * ybuf[k]
    x = x_ref[...] + y
    if apply_norm:
        x = x * lax.rsqrt(jnp.mean(x * x, axis=-1, keepdims=True) + EPS) * g_ref[...]
    o_ref[...] = x


def _combine(dest, gate, x1, gain, ys, tm, *, apply_norm):
    n, d = x1.shape
    return pl.pallas_call(
        functools.partial(_combine_body, tm=tm, apply_norm=apply_norm),
        grid=(n // tm,),
        in_specs=[pl.BlockSpec((1, 1, tm * TOP_K), lambda i: (i, 0, 0), memory_space=pltpu.SMEM),
                  pl.BlockSpec((tm, TOP_K), lambda i: (i, 0)),
                  pl.BlockSpec((tm, d), lambda i: (i, 0)),
                  pl.BlockSpec((1, d), lambda i: (0, 0)),
                  pl.BlockSpec(memory_space=pl.ANY)],
        out_specs=pl.BlockSpec((tm, d), lambda i: (i, 0)),
        out_shape=jax.ShapeDtypeStruct((n, d), F32),
        scratch_shapes=[pltpu.VMEM((TOP_K, tm, d), F32), pltpu.SemaphoreType.DMA(())],
        compiler_params=pltpu.CompilerParams(
            dimension_semantics=("arbitrary",), vmem_limit_bytes=VMEM_LIMIT),
        name="combine_final_norm",
    )(dest.reshape(n // tm, 1, tm * TOP_K), gate, x1, gain, ys)


def _token_tile(n, candidates):
    for c in candidates:
        if n % c == 0:
            return c
    raise ValueError(f"token count {n} has no tile in {candidates}")


def _lane_vector(values, first_lane):
    out = jnp.zeros((1, HEAD_DIM), F32)
    return lax.dynamic_update_slice(out, values.reshape(1, -1).astype(F32), (0, first_lane))


def _route(logits, n_exp):
    n_tok = logits.shape[0]
    top_logit, top_idx = lax.top_k(logits, TOP_K)
    gate = jax.nn.softmax(top_logit, axis=-1)
    n_assign = n_tok * TOP_K
    e_flat = top_idx.reshape(-1)
    onehot = (e_flat[:, None] == jnp.arange(n_exp)[None, :]).astype(jnp.int32)
    seen = jnp.cumsum(onehot, axis=0)
    rank = jnp.sum(onehot * seen, axis=1) - 1
    counts = seen[-1]
    padded = ((counts + MOE_ROWS - 1) // MOE_ROWS) * MOE_ROWS
    pend = jnp.cumsum(padded)
    pstart = pend - padded
    dest = (pstart[e_flat] + rank).astype(jnp.int32).reshape(n_tok, TOP_K)
    n_blocks = -(-n_assign // MOE_ROWS) + n_exp
    block_start = jnp.arange(n_blocks, dtype=pend.dtype) * MOE_ROWS
    block_expert = jnp.minimum(
        jnp.sum(pend[None, :] <= block_start[:, None], axis=1), n_exp - 1).astype(jnp.int32)
    n_used = (pend[-1] // MOE_ROWS).astype(jnp.int32).reshape(1)
    return gate, dest, block_expert, n_used, n_blocks * MOE_ROWS


def kernel(x_prompt, x_sample, state_conv_a, state_delta, state_hgrn, norm_mix, w_in, conv_a, a_log, dt_bias,
           norm_a, lb_logits, norm_b, w_branch, w_out, norm_ffn, w_router, b_router, w_gate_up, b_gate_up,
           w_down, b_down, norm_final):
    bp, tp, d = x_prompt.shape
    bs, ts, _ = x_sample.shape
    depth = w_in.shape[0]
    n_exp = w_router.shape[-1]
    assert d == D_HEADS and tp % BLOCK_ROWS == 0 and ts <= BLOCK_ROWS
    assert min(tp, ts) >= CONV_WIDTH - 1
    d_conv = 3 * D_HEADS
    tail = CONV_WIDTH - 1

    groups = [
        dict(x=x_prompt.reshape(bp * tp, d), batch=bp, t=tp, tin=BLOCK_ROWS, first=True),
        dict(x=x_sample.reshape(bs * ts, d), batch=bs, t=ts, tin=ts, first=False),
    ]
    for g in groups:
        g['n'] = g['batch'] * g['t']
        g['tm_in'] = _token_tile(g['n'], (1024, 768, 512, 256))
        g['tm'] = _token_tile(g['n'], (256,))
        g['conv'], g['delta'], g['hgrn'] = [], [], []

    for l in range(depth):
        w_l = w_in[l]
        w_main = jnp.concatenate([w_l[:, :d_conv], w_l[:, d_conv + 2 * N_HEADS:]], axis=1).astype(BF16)
        w_ba = jnp.pad(w_l[:, d_conv:d_conv + 2 * N_HEADS], ((0, 0), (0, HEAD_DIM - 2 * N_HEADS))).astype(BF16)
        gain = norm_mix[l].reshape(1, d)
        conv_w = jnp.pad(conv_a[l], ((0, CONV_PAD - CONV_WIDTH), (0, 0)))
        alog_l = _lane_vector(a_log[l], N_HEADS)
        dtb_l = _lane_vector(dt_bias[l], N_HEADS)
        na = norm_a[l].reshape(1, HEAD_DIM)
        nb = norm_b[l].reshape(1, HEAD_DIM)
        lbl = jnp.pad(lb_logits.astype(F32), ((0, CONV_PAD - lb_logits.shape[0]), (0, 0)), constant_values=-1e30)
        w_r = jnp.pad(w_router[l], ((0, 0), (0, HEAD_DIM - n_exp)))
        w_r_hi = w_r.astype(BF16)
        w_r = jnp.concatenate([w_r_hi, (w_r - w_r_hi.astype(F32)).astype(BF16), w_r_hi], axis=0)
        b_r = jnp.pad(b_router[l], (0, HEAD_DIM - n_exp)).reshape(1, HEAD_DIM)
        wb0, wb1, wo = w_branch[l, 0].astype(BF16), w_branch[l, 1].astype(BF16), w_out[l].astype(BF16)

        for g in groups:
            batch, t, tin = g['batch'], g['t'], g['tin']
            nt = t // tin
            proj = _inproj(g['x'], gain, w_main, g['tm_in'], D_HEADS)
            ba = _inproj(g['x'], gain, w_ba, g['tm_in'], HEAD_DIM)
            if g['first']:
                cs = jnp.zeros((batch, CONV_PAD, d_conv), F32)
                s_delta = jnp.zeros((batch, N_HEADS, HEAD_DIM, HEAD_DIM), F32)
                s_hgrn_t = s_delta
            else:
                cs = jnp.pad(state_conv_a[l], ((0, 0), (CONV_PAD - tail, 0), (0, 0)))
                s_delta = state_delta[l]
                s_hgrn_t = jnp.swapaxes(state_hgrn[l], -1, -2)
            o_a, sd = _gdn(proj, ba, cs, s_delta, conv_w, alog_l, dtb_l, na, batch=batch, tin=tin, nt=nt)
            o_b, sh = _gla(proj, lbl, nb, s_hgrn_t, batch=batch, tin=tin, nt=nt, layer=l)
            g['conv'].append(jnp.stack([proj[(b + 1) * t - tail:(b + 1) * t, :d_conv] for b in range(batch)]))
            g['delta'].append(sd)
            g['hgrn'].append(jnp.swapaxes(sh, -1, -2))
            g['x1'], g['h'], g['logits'] = _merge(o_a, o_b, proj, g['x'], wb0, wb1, wo,
                                                  norm_ffn[l].reshape(1, d), w_r, b_r, g['tm'])

        logits = jnp.concatenate([g['logits'][:, :n_exp] for g in groups], axis=0)
        gate, dest, block_expert, n_used, n_rows = _route(logits, n_exp)
        xs = jnp.zeros((n_rows, d), F32)
        row = 0
        for g in groups:
            g['dest'] = dest[row:row + g['n']]
            g['gate'] = gate[row:row + g['n']]
            row += g['n']
            xs = _dispatch(g['dest'], g['h'], xs, g['tm'])
        ys = _moe(block_expert, n_used, xs, w_gate_up[l].astype(BF16), b_gate_up[l][:, None, :],
                  w_down[l].astype(BF16), b_down[l][:, None, :])
        for g in groups:
            g['x'] = _combine(g['dest'], g['gate'], g['x1'], norm_final.reshape(1, d), ys, g['tm'],
                              apply_norm=l + 1 == depth)

    gp, gs = groups
    return (gp['x'].reshape(bp, tp, d), gs['x'].reshape(bs, ts, d),
            jnp.stack(gp['conv']), jnp.stack(gp['delta']), jnp.stack(gp['hgrn']),
            jnp.stack(gs['conv']), jnp.stack(gs['delta']), jnp.stack(gs['hgrn']))
```

```python
import functools

import jax
import jax.numpy as jnp
from jax import lax
from jax.experimental import pallas as pl
from jax.experimental.pallas import tpu as pltpu

F32 = jnp.float32
BF16 = jnp.bfloat16

EPS = 1e-6
HEAD_DIM = 128
N_HEADS = 16
D_HEADS = N_HEADS * HEAD_DIM
CONV_WIDTH = 4
CONV_PAD = 8
BLOCK_ROWS = 256
GLA_CHUNK = 64
GLA_SUB = 8
HEAD_GROUP = 2
GDN_HEAD_GROUP = 4
DIAG_UNROLL = True
TOP_K = 4
ROUTER_PAD = -1e30
SWIGLU_LIMIT = 7.0
SWIGLU_ALPHA = 1.702
MOE_ROWS = 512
MOE_FF_TILE = 1024
ROW_COPY_UNROLL = 2
VMEM_LIMIT = 56 * 1024 * 1024

NT_DIMS = (((1,), (1,)), ((), ()))
TN_DIMS = (((0,), (0,)), ((), ()))


def _sigmoid(x):
    return 1.0 / (1.0 + jnp.exp(-x))


def _bdot(a, b):
    return jnp.dot(a.astype(BF16), b.astype(BF16), preferred_element_type=F32)


def _bdot_general(a, b, dims):
    return lax.dot_general(a.astype(BF16), b.astype(BF16), dims, preferred_element_type=F32)


def _split3(x):
    hi = x.astype(BF16)
    r1 = x - hi.astype(F32)
    mid = r1.astype(BF16)
    lo = (r1 - mid.astype(F32)).astype(BF16)
    return hi, mid, lo


def _dot_x01(x, m01):
    m = m01.astype(BF16)
    return jnp.dot(jnp.concatenate(_split3(x), axis=1), jnp.concatenate([m, m, m], axis=0),
                   preferred_element_type=F32)


def _dot_01x(m01, x):
    m = m01.astype(BF16)
    return jnp.dot(jnp.concatenate([m, m, m], axis=1), jnp.concatenate(_split3(x), axis=0),
                   preferred_element_type=F32)


def _dot_01xt(m01, x):
    m = m01.astype(BF16)
    return lax.dot_general(jnp.concatenate([m, m, m], axis=1), jnp.concatenate(_split3(x), axis=1), NT_DIMS,
                           preferred_element_type=F32)


def _trace_round_robin(gens):
    results = [None] * len(gens)
    live = list(range(len(gens)))
    while live:
        for i in list(live):
            try:
                next(gens[i])
            except StopIteration as stop:
                results[i] = stop.value
                live.remove(i)
    return results


def _inproj_body(x_ref, g_ref, w_ref, o_ref, xn_ref):
    @pl.when(pl.program_id(1) == 0)
    def _():
        x = x_ref[...]
        ms = jnp.mean(x * x, axis=-1, keepdims=True)
        xn_ref[...] = (x * lax.rsqrt(ms + EPS) * g_ref[...]).astype(BF16)

    o_ref[...] = jnp.dot(xn_ref[...], w_ref[...], preferred_element_type=F32)


def _inproj(x, gain, w, tm, tn):
    n, d = x.shape
    nc = w.shape[1]
    return pl.pallas_call(
        _inproj_body,
        grid=(n // tm, nc // tn),
        in_specs=[pl.BlockSpec((tm, d), lambda i, j: (i, 0)),
                  pl.BlockSpec((1, d), lambda i, j: (0, 0)),
                  pl.BlockSpec((d, tn), lambda i, j: (0, j))],
        out_specs=pl.BlockSpec((tm, tn), lambda i, j: (i, j)),
        out_shape=jax.ShapeDtypeStruct((n, nc), F32),
        scratch_shapes=[pltpu.VMEM((tm, d), BF16)],
        compiler_params=pltpu.CompilerParams(
            dimension_semantics=("parallel", "arbitrary"), vmem_limit_bytes=VMEM_LIMIT),
        name="inproj",
    )(x, gain, w)


def _compute_rows(tin):
    for rows in (HEAD_DIM, BLOCK_ROWS):
        if tin <= rows:
            return rows
    raise ValueError(f"block of {tin} tokens exceeds {BLOCK_ROWS}")


def _pad_rows(x, rows):
    if x.shape[0] == rows:
        return x
    return jnp.concatenate([x, jnp.zeros((rows - x.shape[0],) + x.shape[1:], x.dtype)], axis=0)


def _gdn_body(qkv_ref, z_ref, ba_ref, cs_ref, s0_ref, cw_ref, alog_ref, dtb_ref, na_ref,
              o_ref, sout_ref, xbuf, s_scr, *, tin, nt, rows):
    R = rows
    t = pl.program_id(1)

    @pl.when(t == 0)
    def _():
        xbuf[0:CONV_PAD, :] = cs_ref[0]
        s_scr[...] = s0_ref[0]

    if nt > 1:
        @pl.when(t > 0)
        def _():
            xbuf[0:CONV_PAD, :] = xbuf[tin:tin + CONV_PAD, :]

    xbuf[CONV_PAD:CONV_PAD + tin, :] = qkv_ref[...]
    if tin < R:
        xbuf[CONV_PAD + tin:CONV_PAD + R, :] = jnp.zeros((R - tin, xbuf.shape[1]), F32)

    rows = lax.broadcasted_iota(jnp.int32, (R, HEAD_DIM), 0)
    lanes = lax.broadcasted_iota(jnp.int32, (R, HEAD_DIM), 1)
    valid = rows < tin
    ti = lax.broadcasted_iota(jnp.int32, (R, R), 0)
    si = lax.broadcasted_iota(jnp.int32, (R, R), 1)
    causal = ti >= si
    merge_key = jnp.where(ti > si, ti ^ si, 0)

    ba = _pad_rows(ba_ref[...], R)
    sp_in = ba + dtb_ref[...]
    softplus = jnp.maximum(sp_in, 0.0) + jnp.log(1.0 + jnp.exp(-jnp.abs(sp_in)))
    g_all = jnp.where(valid, -jnp.exp(alog_ref[...]) * softplus, 0.0)
    beta_all = jnp.where(valid, _sigmoid(ba), 0.0)
    gcum_all = _dot_01x(causal, g_all)
    gate_src = jnp.where(lanes < N_HEADS, beta_all, gcum_all)
    DIAG = HEAD_DIM
    assert R in (DIAG, 2 * DIAG)
    diag_key = merge_key[:DIAG, :DIAG]

    sel_l = lax.broadcasted_iota(jnp.int32, (HEAD_DIM, 2 * HEAD_DIM), 0)
    sel_c = lax.broadcasted_iota(jnp.int32, (HEAD_DIM, 2 * HEAD_DIM), 1)
    lane0 = (lanes == 0).astype(F32)

    def head(h, s_old):
        off = h * HEAD_DIM

        def conv_silu(base):
            cols = pl.ds(pl.multiple_of(base + off, HEAD_DIM), HEAD_DIM)
            acc = None
            for j in range(CONV_WIDTH):
                term = xbuf[pl.ds(CONV_PAD - (CONV_WIDTH - 1) + j, R), cols] * cw_ref[j:j + 1, cols]
                acc = term if acc is None else acc + term
            return acc * _sigmoid(acc)

        qc = conv_silu(0)
        kc = conv_silu(D_HEADS)
        vc = conv_silu(2 * D_HEADS)
        qn = qc * lax.rsqrt(jnp.sum(qc * qc, axis=-1, keepdims=True) + EPS) * (HEAD_DIM ** -0.5)
        kn = jnp.where(valid, kc * lax.rsqrt(jnp.sum(kc * kc, axis=-1, keepdims=True) + EPS), 0.0)

        sel = (((sel_c < HEAD_DIM) & (sel_l == h)) |
               ((sel_c >= HEAD_DIM) & (sel_l == h + N_HEADS))).astype(F32)
        bg = _dot_x01(gate_src, sel)
        yield
        beta = bg[:, :HEAD_DIM]
        gc = bg[:, HEAD_DIM:]
        gc_last = gc[R - 1:R, :]
        gc_t = jnp.concatenate([gc] * (R // HEAD_DIM), axis=1)
        gc_s = _dot_01xt(lane0, gc)
        yield
        decay = jnp.where(causal, jnp.exp(jnp.where(causal, gc_t - gc_s, 0.0)), 0.0)

        kb = kn * beta
        aq = _bdot_general(jnp.concatenate([kb, qn], axis=0), kn, NT_DIMS)
        yield
        a_mat = jnp.where(ti > si, aq[:R] * decay, 0.0)
        qk = aq[R:] * decay

        a_diag = [a_mat[i * DIAG:(i + 1) * DIAG, i * DIAG:(i + 1) * DIAG] for i in range(R // DIAG)]
        n_diag = None
        b = 1
        while b < DIAG:
            level = (diag_key >= b) & (diag_key < 2 * b)
            c_lvl = [jnp.where(level, a, 0.0) for a in a_diag]
            if n_diag is None:
                n_diag = [-c for c in c_lvl]
            else:
                p_lvl = [c + _bdot(n, c) for n, c in zip(n_diag, c_lvl)]
                yield
                n_diag = [n - p - _bdot(p, n) for n, p in zip(n_diag, p_lvl)]
                yield
            b *= 2

        rhs = jnp.concatenate([vc * beta, kb * jnp.exp(gc)], axis=1)
        if len(a_diag) == 1:
            sol = rhs + _bdot(n_diag[0], rhs)
        else:
            n_top, n_bot = n_diag
            a_cross = a_mat[DIAG:, :DIAG]
            p_cross = a_cross + _bdot(n_bot, a_cross)
            yield
            n_cross = -(p_cross + _bdot(p_cross, n_top))
            yield
            rhs_top, rhs_bot = rhs[:DIAG], rhs[DIAG:]
            sol = jnp.concatenate([rhs_top + _bdot(n_top, rhs_top),
                                   rhs_bot + _bdot(n_cross, rhs_top) + _bdot(n_bot, rhs_bot)], axis=0)
        yield
        u = sol[:, :HEAD_DIM]
        w = sol[:, HEAD_DIM:]

        qd = qn * jnp.exp(gc)
        kd = kn * jnp.exp(gc_last - gc)
        ws = _bdot(jnp.concatenate([w, qd], axis=0), s_old)
        yield
        v_new = u - ws[:R]
        o = ws[R:] + _bdot(qk, v_new)
        s_new = s_old * jnp.exp(gc_last) + _bdot_general(kd, v_new, TN_DIMS)
        yield

        cols = pl.ds(pl.multiple_of(off, HEAD_DIM), HEAD_DIM)
        on = o * lax.rsqrt(jnp.mean(o * o, axis=-1, keepdims=True) + EPS) * na_ref[...]
        z = z_ref[:, cols]
        o_ref[:, cols] = (on[:tin] * (z * _sigmoid(z))).astype(BF16)
        return s_new

    def head_group(hg, carry):
        heads = [hg * GDN_HEAD_GROUP + u for u in range(GDN_HEAD_GROUP)]
        states = _trace_round_robin([head(h, s_scr[h]) for h in heads])
        for h, s in zip(heads, states):
            s_scr[h] = s
        return carry

    lax.fori_loop(0, N_HEADS // GDN_HEAD_GROUP, head_group, 0)

    @pl.when(t == nt - 1)
    def _():
        sout_ref[0] = s_scr[...]


def _gdn(proj, ba, conv_state, s0, conv_w, alog_l, dtb_l, norm_a, *, batch, tin, nt):
    rows = _compute_rows(tin)
    body = functools.partial(_gdn_body, tin=tin, nt=nt, rows=rows)
    return pl.pallas_call(
        body,
        grid=(batch, nt),
        in_specs=[
            pl.BlockSpec((tin, 3 * D_HEADS), lambda b, t: (b * nt + t, 0)),
            pl.BlockSpec((tin, D_HEADS), lambda b, t: (b * nt + t, 3)),
            pl.BlockSpec((tin, HEAD_DIM), lambda b, t: (b * nt + t, 0)),
            pl.BlockSpec((1, CONV_PAD, 3 * D_HEADS), lambda b, t: (b, 0, 0)),
            pl.BlockSpec((1, N_HEADS, HEAD_DIM, HEAD_DIM), lambda b, t: (b, 0, 0, 0)),
            pl.BlockSpec((CONV_PAD, 3 * D_HEADS), lambda b, t: (0, 0)),
            pl.BlockSpec((1, HEAD_DIM), lambda b, t: (0, 0)),
            pl.BlockSpec((1, HEAD_DIM), lambda b, t: (0, 0)),
            pl.BlockSpec((1, HEAD_DIM), lambda b, t: (0, 0)),
        ],
        out_specs=[
            pl.BlockSpec((tin, D_HEADS), lambda b, t: (b * nt + t, 0)),
            pl.BlockSpec((1, N_HEADS, HEAD_DIM, HEAD_DIM), lambda b, t: (b, 0, 0, 0)),
        ],
        out_shape=[jax.ShapeDtypeStruct((batch * nt * tin, D_HEADS), BF16),
                   jax.ShapeDtypeStruct((batch, N_HEADS, HEAD_DIM, HEAD_DIM), F32)],
        scratch_shapes=[pltpu.VMEM((rows + CONV_PAD, 3 * D_HEADS), F32),
                        pltpu.VMEM((N_HEADS, HEAD_DIM, HEAD_DIM), F32)],
        compiler_params=pltpu.CompilerParams(
            dimension_semantics=("parallel", "arbitrary"), vmem_limit_bytes=VMEM_LIMIT),
        name="gated_delta",
    )(proj, proj, ba, conv_state, s0, conv_w, alog_l, dtb_l, norm_a)


def _gla_body(q_ref, f_ref, i_ref, z_ref, lbl_ref, nb_ref, s0_ref, o_ref, sout_ref,
              s_scr, *row_scr, tin, nt, layer, rows):
    R = rows
    n_chunks = R // GLA_CHUNK
    t = pl.program_id(1)

    @pl.when(t == 0)
    def _():
        s_scr[...] = s0_ref[0]

    rows = lax.broadcasted_iota(jnp.int32, (R, HEAD_DIM), 0)
    valid = rows < tin
    ti = lax.broadcasted_iota(jnp.int32, (R, R), 0)
    si = lax.broadcasted_iota(jnp.int32, (R, R), 1)
    chunk_tri = ((ti >= si) & ((ti // GLA_CHUNK) == (si // GLA_CHUNK))).astype(F32)
    pair_key = jnp.where(ti > si, ti ^ si, 0)
    row_local = lax.broadcasted_iota(jnp.int32, (GLA_SUB, HEAD_DIM), 0)

    def head_pre(h, slot):
        q_scr, k_scr, v_scr, b_scr, _ = (r.at[slot] for r in row_scr)
        cols = pl.ds(pl.multiple_of(h * HEAD_DIM, HEAD_DIM), HEAD_DIM)
        lbl = lbl_ref[:, cols]
        lb_e = jnp.exp(lbl - jnp.max(lbl, axis=0, keepdims=True))
        lb = jnp.sum(lb_e[0:layer + 1], axis=0, keepdims=True) / jnp.sum(lb_e, axis=0, keepdims=True)

        fl = _pad_rows(f_ref[:, cols], R)
        qx = _pad_rows(q_ref[:, cols], R)
        vv = _pad_rows(i_ref[:, cols], R)
        log_f = jnp.where(valid, jnp.log(lb + (1.0 - lb) * _sigmoid(fl)), 0.0)
        kk = jnp.where(valid, (1.0 - lb) * _sigmoid(-fl), 0.0)
        qq = qx * _sigmoid(qx)
        bc = _dot_01x(chunk_tri, log_f)
        yield
        q_scr[...] = qq
        k_scr[...] = kk
        v_scr[...] = vv
        b_scr[...] = bc

        def block_row(period, row):
            return jnp.concatenate(
                [jnp.broadcast_to(b_scr[start + row:start + row + 1, :], (period, HEAD_DIM))
                 for start in range(0, R, period)], axis=0)

        b_last = block_row(GLA_CHUNK, GLA_CHUNK - 1)
        qd = qq * jnp.exp(bc)
        kd = kk * jnp.exp(b_last - bc)

        scores = jnp.zeros((R, R), F32)
        hs = GLA_SUB
        while hs < GLA_CHUNK:
            b_mid = block_row(2 * hs, hs - 1)
            lower = (rows & hs) != 0
            q_fac = jnp.where(lower, qq * jnp.exp(jnp.minimum(bc - b_mid, 0.0)), 0.0)
            k_fac = jnp.where(lower, 0.0, kk * jnp.exp(jnp.minimum(b_mid - bc, 0.0)))
            level = _bdot_general(q_fac, k_fac, NT_DIMS)
            yield
            scores = jnp.where((pair_key >= hs) & (pair_key < 2 * hs), level, scores)
            hs *= 2
        o_off = _bdot(scores, vv)
        yield
        return qd, kd, vv, o_off

    def diag_block(m, carry):
        r0 = pl.multiple_of(m * GLA_SUB, GLA_SUB)
        for slot in range(HEAD_GROUP):
            q_scr, k_scr, v_scr, b_scr, od_scr = (r.at[slot] for r in row_scr)
            q_blk = q_scr[pl.ds(r0, GLA_SUB), :]
            b_blk = b_scr[pl.ds(r0, GLA_SUB), :]
            acc = jnp.zeros((GLA_SUB, HEAD_DIM), F32)
            for s in range(GLA_SUB):
                b_s = b_scr[pl.ds(r0 + s, 1), :]
                k_s = k_scr[pl.ds(r0 + s, 1), :]
                v_s = v_scr[pl.ds(r0 + s, 1), :]
                e = jnp.where(row_local >= s, jnp.exp(jnp.minimum(b_blk - b_s, 0.0)), 0.0)
                col = jnp.sum(q_blk * k_s * e, axis=-1, keepdims=True)
                acc = acc + col * v_s
            od_scr[pl.ds(r0, GLA_SUB), :] = acc
        return carry

    def head_post(h, slot, pre, s_t):
        b_scr, od_scr = row_scr[3].at[slot], row_scr[4].at[slot]
        qd, kd, vv, o_off = pre
        o_inter = []
        for c in range(n_chunks):
            sl = slice(c * GLA_CHUNK, (c + 1) * GLA_CHUNK)
            o_inter.append(_bdot_general(qd[sl], s_t, NT_DIMS))
            f_last = jnp.exp(b_scr[(c + 1) * GLA_CHUNK - 1:(c + 1) * GLA_CHUNK, :])
            s_t = s_t * f_last + _bdot_general(vv[sl], kd[sl], TN_DIMS)
            yield

        o = jnp.concatenate(o_inter, axis=0) + o_off + od_scr[...]
        on = o * lax.rsqrt(jnp.mean(o * o, axis=-1, keepdims=True) + EPS) * nb_ref[...]
        cols = pl.ds(pl.multiple_of(h * HEAD_DIM, HEAD_DIM), HEAD_DIM)
        z = z_ref[:, cols]
        o_ref[:, cols] = (on[:tin] * _sigmoid(z)).astype(BF16)
        return s_t

    def head_group(hg, carry):
        heads = [hg * HEAD_GROUP + u for u in range(HEAD_GROUP)]
        states = [s_scr[h] for h in heads]
        pre = _trace_round_robin([head_pre(h, u) for u, h in enumerate(heads)])
        lax.fori_loop(0, R // GLA_SUB, diag_block, 0, unroll=DIAG_UNROLL)
        states = _trace_round_robin([head_post(h, u, p, s) for u, (h, p, s) in enumerate(zip(heads, pre, states))])
        for h, s in zip(heads, states):
            s_scr[h] = s
        return carry

    lax.fori_loop(0, N_HEADS // HEAD_GROUP, head_group, 0)

    @pl.when(t == nt - 1)
    def _():
        sout_ref[0] = s_scr[...]


def _gla(proj, lb_logits, norm_b, s0_t, *, batch, tin, nt, layer):
    rows = _compute_rows(tin)
    body = functools.partial(_gla_body, tin=tin, nt=nt, layer=layer, rows=rows)
    col_spec = lambda c: pl.BlockSpec((tin, D_HEADS), lambda b, t: (b * nt + t, c))
    head_rows = pltpu.VMEM((HEAD_GROUP, rows, HEAD_DIM), F32)
    return pl.pallas_call(
        body,
        grid=(batch, nt),
        in_specs=[
            col_spec(4), col_spec(5), col_spec(6), col_spec(7),
            pl.BlockSpec(lb_logits.shape, lambda b, t: (0, 0)),
            pl.BlockSpec((1, HEAD_DIM), lambda b, t: (0, 0)),
            pl.BlockSpec((1, N_HEADS, HEAD_DIM, HEAD_DIM), lambda b, t: (b, 0, 0, 0)),
        ],
        out_specs=[
            pl.BlockSpec((tin, D_HEADS), lambda b, t: (b * nt + t, 0)),
            pl.BlockSpec((1, N_HEADS, HEAD_DIM, HEAD_DIM), lambda b, t: (b, 0, 0, 0)),
        ],
        out_shape=[jax.ShapeDtypeStruct((batch * nt * tin, D_HEADS), BF16),
                   jax.ShapeDtypeStruct((batch, N_HEADS, HEAD_DIM, HEAD_DIM), F32)],
        scratch_shapes=[pltpu.VMEM((N_HEADS, HEAD_DIM, HEAD_DIM), F32),
                        head_rows, head_rows, head_rows, head_rows, head_rows],
        compiler_params=pltpu.CompilerParams(
            dimension_semantics=("parallel", "arbitrary"), vmem_limit_bytes=VMEM_LIMIT),
        name="hgrn2",
    )(proj, proj, proj, proj, lb_logits, norm_b, s0_t)


def _merge_body(oa_ref, ob_ref, g0_ref, g1_ref, x_ref, wb0_ref, wb1_ref, wo_ref, nf_ref, wr_ref, br_ref,
                x1_ref, h_ref, gate_ref, idx_ref):
    ua = jnp.dot(oa_ref[...], wb0_ref[...], preferred_element_type=F32)
    ub = jnp.dot(ob_ref[...], wb1_ref[...], preferred_element_type=F32)
    merged = _sigmoid(g0_ref[...]) * ua + _sigmoid(g1_ref[...]) * ub
    x1 = x_ref[...] + jnp.dot(merged.astype(BF16), wo_ref[...], preferred_element_type=F32)
    x1_ref[...] = x1
    hn = x1 * lax.rsqrt(jnp.mean(x1 * x1, axis=-1, keepdims=True) + EPS) * nf_ref[...]
    h_ref[...] = hn
    h_hi = hn.astype(BF16)
    h_lo = (hn - h_hi.astype(F32)).astype(BF16)
    logits = jnp.dot(jnp.concatenate([h_hi, h_hi, h_lo], axis=1), wr_ref[...],
                     preferred_element_type=F32) + br_ref[...]

    lane = lax.broadcasted_iota(jnp.int32, logits.shape, 1)
    gates = jnp.zeros(logits.shape, F32)
    picks = jnp.zeros(logits.shape, jnp.int32)
    top = None
    for k in range(TOP_K):
        best = jnp.max(logits, axis=-1, keepdims=True)
        pick = jnp.min(jnp.where(logits == best, lane, HEAD_DIM), axis=-1, keepdims=True)
        top = best if top is None else top
        gates = jnp.where(lane == k, jnp.exp(best - top), gates)
        picks = jnp.where(lane == k, pick, picks)
        logits = jnp.where(lane == pick, ROUTER_PAD, logits)
    gate_ref[...] = gates / jnp.sum(gates, axis=-1, keepdims=True)
    idx_ref[...] = picks


def _merge(o_a, o_b, proj, x, wb0, wb1, wo, norm_ffn, w_router, b_router, tm):
    n, d = x.shape
    const = lambda shape: pl.BlockSpec(shape, lambda i: (0, 0), pipeline_mode=pl.Buffered(1))
    return pl.pallas_call(
        _merge_body,
        grid=(n // tm,),
        in_specs=[
            pl.BlockSpec((tm, D_HEADS), lambda i: (i, 0)),
            pl.BlockSpec((tm, D_HEADS), lambda i: (i, 0)),
            pl.BlockSpec((tm, d), lambda i: (i, 8)),
            pl.BlockSpec((tm, d), lambda i: (i, 9)),
            pl.BlockSpec((tm, d), lambda i: (i, 0)),
            const(wb0.shape), const(wb1.shape), const(wo.shape),
            const((1, d)), const(w_router.shape), const((1, HEAD_DIM)),
        ],
        out_specs=[pl.BlockSpec((tm, d), lambda i: (i, 0)),
                   pl.BlockSpec((tm, d), lambda i: (i, 0)),
                   pl.BlockSpec((tm, HEAD_DIM), lambda i: (i, 0)),
                   pl.BlockSpec((tm, HEAD_DIM), lambda i: (i, 0))],
        out_shape=[jax.ShapeDtypeStruct((n, d), F32),
                   jax.ShapeDtypeStruct((n, d), F32),
                   jax.ShapeDtypeStruct((n, HEAD_DIM), F32),
                   jax.ShapeDtypeStruct((n, HEAD_DIM), jnp.int32)],
        compiler_params=pltpu.CompilerParams(
            dimension_semantics=("parallel",), vmem_limit_bytes=VMEM_LIMIT),
        name="merge_out",
    )(o_a, o_b, proj, proj, x, wb0, wb1, wo, norm_ffn, w_router, b_router)


def _moe_body(be_ref, nu_ref, x_ref, wg_ref, wu_ref, bg_ref, bu_ref, wd_ref, bd_ref, o_ref):
    i = pl.program_id(0)
    f = pl.program_id(1)

    @pl.when(i < nu_ref[0])
    def _():
        x = x_ref[...].astype(BF16)
        gate = jnp.dot(x, wg_ref[0], preferred_element_type=F32) + bg_ref[0]
        up = jnp.dot(x, wu_ref[0], preferred_element_type=F32) + bu_ref[0]
        gate = jnp.minimum(gate, SWIGLU_LIMIT)
        up = jnp.clip(up, -SWIGLU_LIMIT, SWIGLU_LIMIT)
        act = (up + 1.0) * gate * _sigmoid(SWIGLU_ALPHA * gate)
        part = jnp.dot(act.astype(BF16), wd_ref[0], preferred_element_type=F32)

        @pl.when(f == 0)
        def _():
            o_ref[...] = part + bd_ref[0]

        @pl.when(f > 0)
        def _():
            o_ref[...] += part


def _moe(block_expert, n_used, xs, w_gu, b_gu, w_d, b_d):
    n_rows, d = xs.shape
    n_exp, _, two_ff = w_gu.shape
    d_ff = two_ff // 2
    tf = min(MOE_FF_TILE, d_ff)
    nf = d_ff // tf
    nblk = n_rows // MOE_ROWS
    grid_spec = pltpu.PrefetchScalarGridSpec(
        num_scalar_prefetch=2,
        grid=(nblk, nf),
        in_specs=[
            pl.BlockSpec((MOE_ROWS, d), lambda i, f, be, nu: (i, 0)),
            pl.BlockSpec((1, d, tf), lambda i, f, be, nu: (be[i], 0, f)),
            pl.BlockSpec((1, d, tf), lambda i, f, be, nu: (be[i], 0, nf + f)),
            pl.BlockSpec((1, 1, tf), lambda i, f, be, nu: (be[i], 0, f)),
            pl.BlockSpec((1, 1, tf), lambda i, f, be, nu: (be[i], 0, nf + f)),
            pl.BlockSpec((1, tf, d), lambda i, f, be, nu: (be[i], f, 0)),
            pl.BlockSpec((1, 1, d), lambda i, f, be, nu: (be[i], 0, 0)),
        ],
        out_specs=pl.BlockSpec((MOE_ROWS, d), lambda i, f, be, nu: (i, 0)),
    )
    return pl.pallas_call(
        _moe_body,
        grid_spec=grid_spec,
        out_shape=jax.ShapeDtypeStruct((n_rows, d), F32),
        compiler_params=pltpu.CompilerParams(
            dimension_semantics=("parallel", "arbitrary"), vmem_limit_bytes=VMEM_LIMIT),
        name="expert_mlp",
    )(block_expert, n_used, xs, w_gu, w_gu, b_gu, b_gu, w_d, b_d)


def _row_copy(src_ref, src_row, dst_ref, dst_row, sem):
    return pltpu.make_async_copy(src_ref.at[pl.ds(src_row, 1)], dst_ref.at[pl.ds(dst_row, 1)], sem)


def _dispatch_body(dest_ref, h_ref, xs_in_ref, xs_ref, sem, *, tm):
    del xs_in_ref

    def start(r, carry):
        for k in range(TOP_K):
            _row_copy(h_ref, r, xs_ref, dest_ref[0, 0, r * TOP_K + k], sem).start()
        return carry

    def wait(r, carry):
        for _ in range(TOP_K):
            _row_copy(h_ref, 0, xs_ref, 0, sem).wait()
        return carry

    lax.fori_loop(0, tm, start, 0, unroll=ROW_COPY_UNROLL)
    lax.fori_loop(0, tm, wait, 0, unroll=ROW_COPY_UNROLL)


def _dispatch(dest, h, xs, tm):
    n, d = h.shape
    n_rows = xs.shape[0]
    return pl.pallas_call(
        functools.partial(_dispatch_body, tm=tm),
        grid=(n // tm,),
        in_specs=[pl.BlockSpec((1, 1, tm * TOP_K), lambda i: (i, 0, 0), memory_space=pltpu.SMEM),
                  pl.BlockSpec((tm, d), lambda i: (i, 0)),
                  pl.BlockSpec(memory_space=pl.ANY)],
        out_specs=pl.BlockSpec(memory_space=pl.ANY),
        out_shape=jax.ShapeDtypeStruct((n_rows, d), F32),
        scratch_shapes=[pltpu.SemaphoreType.DMA(())],
        input_output_aliases={2: 0},
        compiler_params=pltpu.CompilerParams(
            dimension_semantics=("arbitrary",), vmem_limit_bytes=VMEM_LIMIT),
        name="dispatch_rows",
    )(dest.reshape(n // tm, 1, tm * TOP_K), h, xs)


def _combine_body(dest_ref, gate_ref, x_ref, g_ref, ys_ref, o_ref, ybuf, sem, *, tm, apply_norm):
    def start(r, carry):
        for k in range(TOP_K):
            _row_copy(ys_ref, dest_ref[0, 0, r * TOP_K + k], ybuf.at[k], r, sem).start()
        return carry

    def wait(r, carry):
        for k in range(TOP_K):
            _row_copy(ys_ref, 0, ybuf.at[k], 0, sem).wait()
        return carry

    lax.fori_loop(0, tm, start, 0, unroll=ROW_COPY_UNROLL)
    lax.fori_loop(0, tm, wait, 0, unroll=ROW_COPY_UNROLL)

    gate = gate_ref[...]
    y = gate[:, 0:1] * ybuf[0]
    for k in range(1, TOP_K):
        y = y + gate[:, k:k + 1] * ybuf[k]
    x = x_ref[...] + y
    if apply_norm:
        x = x * lax.rsqrt(jnp.mean(x * x, axis=-1, keepdims=True) + EPS) * g_ref[...]
    o_ref[...] = x


def _combine(dest, gate, x1, gain, ys, tm, *, apply_norm):
    n, d = x1.shape
    return pl.pallas_call(
        functools.partial(_combine_body, tm=tm, apply_norm=apply_norm),
        grid=(n // tm,),
        in_specs=[pl.BlockSpec((1, 1, tm * TOP_K), lambda i: (i, 0, 0), memory_space=pltpu.SMEM),
                  pl.BlockSpec((tm, TOP_K), lambda i: (i, 0)),
                  pl.BlockSpec((tm, d), lambda i: (i, 0)),
                  pl.BlockSpec((1, d), lambda i: (0, 0)),
                  pl.BlockSpec(memory_space=pl.ANY)],
        out_specs=pl.BlockSpec((tm, d), lambda i: (i, 0)),
        out_shape=jax.ShapeDtypeStruct((n, d), F32),
        scratch_shapes=[pltpu.VMEM((TOP_K, tm, d), F32), pltpu.SemaphoreType.DMA(())],
        compiler_params=pltpu.CompilerParams(
            dimension_semantics=("arbitrary",), vmem_limit_bytes=VMEM_LIMIT),
        name="combine_final_norm",
    )(dest.reshape(n // tm, 1, tm * TOP_K), gate, x1, gain, ys)


def _token_tile(n, candidates):
    for c in candidates:
        if n % c == 0:
            return c
    raise ValueError(f"token count {n} has no tile in {candidates}")


def _lane_vector(values, first_lane):
    out = jnp.zeros((1, HEAD_DIM), F32)
    return lax.dynamic_update_slice(out, values.reshape(1, -1).astype(F32), (0, first_lane))


def _route(top_idx, n_exp):
    n_tok = top_idx.shape[0]
    n_assign = n_tok * TOP_K
    e_flat = top_idx.reshape(-1)
    onehot = (e_flat[:, None] == jnp.arange(n_exp)[None, :]).astype(jnp.int32)
    seen = jnp.cumsum(onehot, axis=0)
    rank = jnp.sum(onehot * seen, axis=1) - 1
    counts = seen[-1]
    padded = ((counts + MOE_ROWS - 1) // MOE_ROWS) * MOE_ROWS
    pend = jnp.cumsum(padded)
    pstart = pend - padded
    dest = (pstart[e_flat] + rank).astype(jnp.int32).reshape(n_tok, TOP_K)
    n_blocks = -(-n_assign // MOE_ROWS) + n_exp
    block_start = jnp.arange(n_blocks, dtype=pend.dtype) * MOE_ROWS
    block_expert = jnp.minimum(
        jnp.sum(pend[None, :] <= block_start[:, None], axis=1), n_exp - 1).astype(jnp.int32)
    n_used = (pend[-1] // MOE_ROWS).astype(jnp.int32).reshape(1)
    return dest, block_expert, n_used, n_blocks * MOE_ROWS


def kernel(x_prompt, x_sample, state_conv_a, state_delta, state_hgrn, norm_mix, w_in, conv_a, a_log, dt_bias,
           norm_a, lb_logits, norm_b, w_branch, w_out, norm_ffn, w_router, b_router, w_gate_up, b_gate_up,
           w_down, b_down, norm_final):
    bp, tp, d = x_prompt.shape
    bs, ts, _ = x_sample.shape
    depth = w_in.shape[0]
    n_exp = w_router.shape[-1]
    assert d == D_HEADS and tp % BLOCK_ROWS == 0 and ts <= BLOCK_ROWS
    assert min(tp, ts) >= CONV_WIDTH - 1
    d_conv = 3 * D_HEADS
    tail = CONV_WIDTH - 1

    groups = [
        dict(x=x_prompt.reshape(bp * tp, d), batch=bp, t=tp, tin=BLOCK_ROWS, first=True),
        dict(x=x_sample.reshape(bs * ts, d), batch=bs, t=ts, tin=ts, first=False),
    ]
    for g in groups:
        g['n'] = g['batch'] * g['t']
        g['tm_in'] = _token_tile(g['n'], (1024, 768, 512, 256))
        g['tm'] = _token_tile(g['n'], (256,))
        g['conv'], g['delta'], g['hgrn'] = [], [], []

    for l in range(depth):
        w_l = w_in[l]
        w_main = jnp.concatenate([w_l[:, :d_conv], w_l[:, d_conv + 2 * N_HEADS:]], axis=1).astype(BF16)
        w_ba = jnp.pad(w_l[:, d_conv:d_conv + 2 * N_HEADS], ((0, 0), (0, HEAD_DIM - 2 * N_HEADS))).astype(BF16)
        gain = norm_mix[l].reshape(1, d)
        conv_w = jnp.pad(conv_a[l], ((0, CONV_PAD - CONV_WIDTH), (0, 0)))
        alog_l = _lane_vector(a_log[l], N_HEADS)
        dtb_l = _lane_vector(dt_bias[l], N_HEADS)
        na = norm_a[l].reshape(1, HEAD_DIM)
        nb = norm_b[l].reshape(1, HEAD_DIM)
        lbl = jnp.pad(lb_logits.astype(F32), ((0, CONV_PAD - lb_logits.shape[0]), (0, 0)), constant_values=-1e30)
        w_r = jnp.pad(w_router[l], ((0, 0), (0, HEAD_DIM - n_exp)))
        w_r_hi = w_r.astype(BF16)
        w_r = jnp.concatenate([w_r_hi, (w_r - w_r_hi.astype(F32)).astype(BF16), w_r_hi], axis=0)
        b_r = jnp.pad(b_router[l], (0, HEAD_DIM - n_exp), constant_values=ROUTER_PAD).reshape(1, HEAD_DIM)
        wb0, wb1, wo = w_branch[l, 0].astype(BF16), w_branch[l, 1].astype(BF16), w_out[l].astype(BF16)

        for g in groups:
            batch, t, tin = g['batch'], g['t'], g['tin']
            nt = t // tin
            proj = _inproj(g['x'], gain, w_main, g['tm_in'], D_HEADS)
            ba = _inproj(g['x'], gain, w_ba, g['tm_in'], HEAD_DIM)
            if g['first']:
                cs = jnp.zeros((batch, CONV_PAD, d_conv), F32)
                s_delta = jnp.zeros((batch, N_HEADS, HEAD_DIM, HEAD_DIM), F32)
                s_hgrn_t = s_delta
            else:
                cs = jnp.pad(state_conv_a[l], ((0, 0), (CONV_PAD - tail, 0), (0, 0)))
                s_delta = state_delta[l]
                s_hgrn_t = jnp.swapaxes(state_hgrn[l], -1, -2)
            o_a, sd = _gdn(proj, ba, cs, s_delta, conv_w, alog_l, dtb_l, na, batch=batch, tin=tin, nt=nt)
            o_b, sh = _gla(proj, lbl, nb, s_hgrn_t, batch=batch, tin=tin, nt=nt, layer=l)
            g['conv'].append(jnp.stack([proj[(b + 1) * t - tail:(b + 1) * t, :d_conv] for b in range(batch)]))
            g['delta'].append(sd)
            g['hgrn'].append(jnp.swapaxes(sh, -1, -2))
            g['x1'], g['h'], g['gates'], g['top_idx'] = _merge(o_a, o_b, proj, g['x'], wb0, wb1, wo,
                                                  norm_ffn[l].reshape(1, d), w_r, b_r, g['tm'])

        top_idx = jnp.concatenate([g['top_idx'][:, :TOP_K] for g in groups], axis=0)
        dest, block_expert, n_used, n_rows = _route(top_idx, n_exp)
        xs = jnp.zeros((n_rows, d), F32)
        row = 0
        for g in groups:
            g['dest'] = dest[row:row + g['n']]
            g['gate'] = g['gates'][:, :TOP_K]
            row += g['n']
            xs = _dispatch(g['dest'], g['h'], xs, g['tm'])
        ys = _moe(block_expert, n_used, xs, w_gate_up[l].astype(BF16), b_gate_up[l][:, None, :],
                  w_down[l].astype(BF16), b_down[l][:, None, :])
        for g in groups:
            g['x'] = _combine(g['dest'], g['gate'], g['x1'], norm_final.reshape(1, d), ys, g['tm'],
                              apply_norm=l + 1 == depth)

    gp, gs = groups
    return (gp['x'].reshape(bp, tp, d), gs['x'].reshape(bs, ts, d),
            jnp.stack(gp['conv']), jnp.stack(gp['delta']), jnp.stack(gp['hgrn']),
            jnp.stack(gs['conv']), jnp.stack(gs['delta']), jnp.stack(gs['hgrn']))
```

```python
import functools

import jax
import jax.numpy as jnp
from jax import lax
from jax.experimental import pallas as pl
from jax.experimental.pallas import tpu as pltpu

F32 = jnp.float32
BF16 = jnp.bfloat16

EPS = 1e-6
HEAD_DIM = 128
N_HEADS = 16
D_HEADS = N_HEADS * HEAD_DIM
CONV_WIDTH = 4
CONV_PAD = 8
BLOCK_ROWS = 256
GLA_CHUNK = 64
GLA_SUB = 8
HEAD_GROUP = 2
GDN_HEAD_GROUP = 4
DIAG_UNROLL = True
TOP_K = 4
ROUTER_PAD = -1e30
SWIGLU_LIMIT = 7.0
SWIGLU_ALPHA = 1.702
MOE_ROWS = 512
MOE_FF_TILE = 1024
ROW_COPY_UNROLL = 2
VMEM_LIMIT = 56 * 1024 * 1024

NT_DIMS = (((1,), (1,)), ((), ()))
TN_DIMS = (((0,), (0,)), ((), ()))


def _sigmoid(x):
    return 1.0 / (1.0 + jnp.exp(-x))


def _bdot(a, b):
    return jnp.dot(a.astype(BF16), b.astype(BF16), preferred_element_type=F32)


def _bdot_general(a, b, dims):
    return lax.dot_general(a.astype(BF16), b.astype(BF16), dims, preferred_element_type=F32)


def _split3(x):
    hi = x.astype(BF16)
    r1 = x - hi.astype(F32)
    mid = r1.astype(BF16)
    lo = (r1 - mid.astype(F32)).astype(BF16)
    return hi, mid, lo


def _dot_x01(x, m01):
    m = m01.astype(BF16)
    return jnp.dot(jnp.concatenate(_split3(x), axis=1), jnp.concatenate([m, m, m], axis=0),
                   preferred_element_type=F32)


def _dot_01x(m01, x):
    m = m01.astype(BF16)
    return jnp.dot(jnp.concatenate([m, m, m], axis=1), jnp.concatenate(_split3(x), axis=0),
                   preferred_element_type=F32)


def _dot_01xt(m01, x):
    m = m01.astype(BF16)
    return lax.dot_general(jnp.concatenate([m, m, m], axis=1), jnp.concatenate(_split3(x), axis=1), NT_DIMS,
                           preferred_element_type=F32)


def _trace_round_robin(gens):
    results = [None] * len(gens)
    live = list(range(len(gens)))
    while live:
        for i in list(live):
            try:
                next(gens[i])
            except StopIteration as stop:
                results[i] = stop.value
                live.remove(i)
    return results


def _inproj_body(x_ref, g_ref, w_ref, o_ref, xn_ref):
    @pl.when(pl.program_id(1) == 0)
    def _():
        x = x_ref[...]
        ms = jnp.mean(x * x, axis=-1, keepdims=True)
        xn_ref[...] = (x * lax.rsqrt(ms + EPS) * g_ref[...]).astype(BF16)

    o_ref[...] = jnp.dot(xn_ref[...], w_ref[...], preferred_element_type=F32)


def _inproj(x, gain, w, tm, tn):
    n, d = x.shape
    nc = w.shape[1]
    return pl.pallas_call(
        _inproj_body,
        grid=(n // tm, nc // tn),
        in_specs=[pl.BlockSpec((tm, d), lambda i, j: (i, 0)),
                  pl.BlockSpec((1, d), lambda i, j: (0, 0)),
                  pl.BlockSpec((d, tn), lambda i, j: (0, j))],
        out_specs=pl.BlockSpec((tm, tn), lambda i, j: (i, j)),
        out_shape=jax.ShapeDtypeStruct((n, nc), F32),
        scratch_shapes=[pltpu.VMEM((tm, d), BF16)],
        compiler_params=pltpu.CompilerParams(
            dimension_semantics=("parallel", "arbitrary"), vmem_limit_bytes=VMEM_LIMIT),
        name="inproj",
    )(x, gain, w)


def _compute_rows(tin):
    for rows in (HEAD_DIM, BLOCK_ROWS):
        if tin <= rows:
            return rows
    raise ValueError(f"block of {tin} tokens exceeds {BLOCK_ROWS}")


def _pad_rows(x, rows):
    if x.shape[0] == rows:
        return x
    return jnp.concatenate([x, jnp.zeros((rows - x.shape[0],) + x.shape[1:], x.dtype)], axis=0)


def _gdn_body(qkv_ref, z_ref, ba_ref, cs_ref, s0_ref, cw_ref, alog_ref, dtb_ref, na_ref,
              o_ref, sout_ref, xbuf, s_scr, *, tin, nt, rows):
    R = rows
    t = pl.program_id(1)

    @pl.when(t == 0)
    def _():
        xbuf[0:CONV_PAD, :] = cs_ref[0]
        s_scr[...] = s0_ref[0]

    if nt > 1:
        @pl.when(t > 0)
        def _():
            xbuf[0:CONV_PAD, :] = xbuf[tin:tin + CONV_PAD, :]

    xbuf[CONV_PAD:CONV_PAD + tin, :] = qkv_ref[...]
    if tin < R:
        xbuf[CONV_PAD + tin:CONV_PAD + R, :] = jnp.zeros((R - tin, xbuf.shape[1]), F32)

    rows = lax.broadcasted_iota(jnp.int32, (R, HEAD_DIM), 0)
    lanes = lax.broadcasted_iota(jnp.int32, (R, HEAD_DIM), 1)
    valid = rows < tin
    ti = lax.broadcasted_iota(jnp.int32, (R, R), 0)
    si = lax.broadcasted_iota(jnp.int32, (R, R), 1)
    causal = ti >= si
    merge_key = jnp.where(ti > si, ti ^ si, 0)

    ba = _pad_rows(ba_ref[...], R)
    sp_in = ba + dtb_ref[...]
    softplus = jnp.maximum(sp_in, 0.0) + jnp.log(1.0 + jnp.exp(-jnp.abs(sp_in)))
    g_all = jnp.where(valid, -jnp.exp(alog_ref[...]) * softplus, 0.0)
    beta_all = jnp.where(valid, _sigmoid(ba), 0.0)
    gcum_all = _dot_01x(causal, g_all)
    gate_src = jnp.where(lanes < N_HEADS, beta_all, gcum_all)
    DIAG = HEAD_DIM
    assert R in (DIAG, 2 * DIAG)
    diag_key = merge_key[:DIAG, :DIAG]

    sel_l = lax.broadcasted_iota(jnp.int32, (HEAD_DIM, 2 * HEAD_DIM), 0)
    sel_c = lax.broadcasted_iota(jnp.int32, (HEAD_DIM, 2 * HEAD_DIM), 1)
    lane0 = (lanes == 0).astype(F32)

    def head(h, s_old):
        off = h * HEAD_DIM

        def conv_silu(base):
            cols = pl.ds(pl.multiple_of(base + off, HEAD_DIM), HEAD_DIM)
            acc = None
            for j in range(CONV_WIDTH):
                term = xbuf[pl.ds(CONV_PAD - (CONV_WIDTH - 1) + j, R), cols] * cw_ref[j:j + 1, cols]
                acc = term if acc is None else acc + term
            return acc * _sigmoid(acc)

        qc = conv_silu(0)
        kc = conv_silu(D_HEADS)
        vc = conv_silu(2 * D_HEADS)
        qn = qc * lax.rsqrt(jnp.sum(qc * qc, axis=-1, keepdims=True) + EPS) * (HEAD_DIM ** -0.5)
        kn = jnp.where(valid, kc * lax.rsqrt(jnp.sum(kc * kc, axis=-1, keepdims=True) + EPS), 0.0)

        sel = (((sel_c < HEAD_DIM) & (sel_l == h)) |
               ((sel_c >= HEAD_DIM) & (sel_l == h + N_HEADS))).astype(F32)
        bg = _dot_x01(gate_src, sel)
        yield
        beta = bg[:, :HEAD_DIM]
        gc = bg[:, HEAD_DIM:]
        gc_last = gc[R - 1:R, :]
        gc_t = jnp.concatenate([gc] * (R // HEAD_DIM), axis=1)
        gc_s = _dot_01xt(lane0, gc)
        yield
        decay = jnp.where(causal, jnp.exp(jnp.where(causal, gc_t - gc_s, 0.0)), 0.0)

        kb = kn * beta
        aq = _bdot_general(jnp.concatenate([kb, qn], axis=0), kn, NT_DIMS)
        yield
        a_mat = jnp.where(ti > si, aq[:R] * decay, 0.0)
        qk = aq[R:] * decay

        a_diag = [a_mat[i * DIAG:(i + 1) * DIAG, i * DIAG:(i + 1) * DIAG] for i in range(R // DIAG)]
        n_diag = None
        b = 1
        while b < DIAG:
            level = (diag_key >= b) & (diag_key < 2 * b)
            c_lvl = [jnp.where(level, a, 0.0) for a in a_diag]
            if n_diag is None:
                n_diag = [-c for c in c_lvl]
            else:
                p_lvl = [c + _bdot(n, c) for n, c in zip(n_diag, c_lvl)]
                yield
                n_diag = [n - p - _bdot(p, n) for n, p in zip(n_diag, p_lvl)]
                yield
            b *= 2

        rhs = jnp.concatenate([vc * beta, kb * jnp.exp(gc)], axis=1)
        if len(a_diag) == 1:
            sol = rhs + _bdot(n_diag[0], rhs)
        else:
            n_top, n_bot = n_diag
            a_cross = a_mat[DIAG:, :DIAG]
            p_cross = a_cross + _bdot(n_bot, a_cross)
            yield
            n_cross = -(p_cross + _bdot(p_cross, n_top))
            yield
            rhs_top, rhs_bot = rhs[:DIAG], rhs[DIAG:]
            sol = jnp.concatenate([rhs_top + _bdot(n_top, rhs_top),
                                   rhs_bot + _bdot(n_cross, rhs_top) + _bdot(n_bot, rhs_bot)], axis=0)
        yield
        u = sol[:, :HEAD_DIM]
        w = sol[:, HEAD_DIM:]

        qd = qn * jnp.exp(gc)
        kd = kn * jnp.exp(gc_last - gc)
        ws = _bdot(jnp.concatenate([w, qd], axis=0), s_old)
        yield
        v_new = u - ws[:R]
        o = ws[R:] + _bdot(qk, v_new)
        s_new = s_old * jnp.exp(gc_last) + _bdot_general(kd, v_new, TN_DIMS)
        yield

        cols = pl.ds(pl.multiple_of(off, HEAD_DIM), HEAD_DIM)
        on = o * lax.rsqrt(jnp.mean(o * o, axis=-1, keepdims=True) + EPS) * na_ref[...]
        z = z_ref[:, cols]
        o_ref[:, cols] = (on[:tin] * (z * _sigmoid(z))).astype(BF16)
        return s_new

    def head_group(hg, carry):
        heads = [hg * GDN_HEAD_GROUP + u for u in range(GDN_HEAD_GROUP)]
        states = _trace_round_robin([head(h, s_scr[h]) for h in heads])
        for h, s in zip(heads, states):
            s_scr[h] = s
        return carry

    lax.fori_loop(0, N_HEADS // GDN_HEAD_GROUP, head_group, 0)

    @pl.when(t == nt - 1)
    def _():
        sout_ref[0] = s_scr[...]


def _gdn(proj, ba, conv_state, s0, conv_w, alog_l, dtb_l, norm_a, *, batch, tin, nt):
    rows = _compute_rows(tin)
    body = functools.partial(_gdn_body, tin=tin, nt=nt, rows=rows)
    return pl.pallas_call(
        body,
        grid=(batch, nt),
        in_specs=[
            pl.BlockSpec((tin, 3 * D_HEADS), lambda b, t: (b * nt + t, 0)),
            pl.BlockSpec((tin, D_HEADS), lambda b, t: (b * nt + t, 3)),
            pl.BlockSpec((tin, HEAD_DIM), lambda b, t: (b * nt + t, 0)),
            pl.BlockSpec((1, CONV_PAD, 3 * D_HEADS), lambda b, t: (b, 0, 0)),
            pl.BlockSpec((1, N_HEADS, HEAD_DIM, HEAD_DIM), lambda b, t: (b, 0, 0, 0)),
            pl.BlockSpec((CONV_PAD, 3 * D_HEADS), lambda b, t: (0, 0)),
            pl.BlockSpec((1, HEAD_DIM), lambda b, t: (0, 0)),
            pl.BlockSpec((1, HEAD_DIM), lambda b, t: (0, 0)),
            pl.BlockSpec((1, HEAD_DIM), lambda b, t: (0, 0)),
        ],
        out_specs=[
            pl.BlockSpec((tin, D_HEADS), lambda b, t: (b * nt + t, 0)),
            pl.BlockSpec((1, N_HEADS, HEAD_DIM, HEAD_DIM), lambda b, t: (b, 0, 0, 0)),
        ],
        out_shape=[jax.ShapeDtypeStruct((batch * nt * tin, D_HEADS), BF16),
                   jax.ShapeDtypeStruct((batch, N_HEADS, HEAD_DIM, HEAD_DIM), F32)],
        scratch_shapes=[pltpu.VMEM((rows + CONV_PAD, 3 * D_HEADS), F32),
                        pltpu.VMEM((N_HEADS, HEAD_DIM, HEAD_DIM), F32)],
        compiler_params=pltpu.CompilerParams(
            dimension_semantics=("parallel", "arbitrary"), vmem_limit_bytes=VMEM_LIMIT),
        name="gated_delta",
    )(proj, proj, ba, conv_state, s0, conv_w, alog_l, dtb_l, norm_a)


def _gla_body(q_ref, f_ref, i_ref, z_ref, lbl_ref, nb_ref, s0_ref, o_ref, sout_ref,
              s_scr, *row_scr, tin, nt, layer, rows):
    R = rows
    n_chunks = R // GLA_CHUNK
    t = pl.program_id(1)

    @pl.when(t == 0)
    def _():
        s_scr[...] = s0_ref[0]

    rows = lax.broadcasted_iota(jnp.int32, (R, HEAD_DIM), 0)
    valid = rows < tin
    ti = lax.broadcasted_iota(jnp.int32, (R, R), 0)
    si = lax.broadcasted_iota(jnp.int32, (R, R), 1)
    chunk_tri = ((ti >= si) & ((ti // GLA_CHUNK) == (si // GLA_CHUNK))).astype(F32)
    pair_key = jnp.where(ti > si, ti ^ si, 0)
    row_local = lax.broadcasted_iota(jnp.int32, (GLA_SUB, HEAD_DIM), 0)

    def head_pre(h, slot):
        q_scr, k_scr, v_scr, b_scr, _ = (r.at[slot] for r in row_scr)
        cols = pl.ds(pl.multiple_of(h * HEAD_DIM, HEAD_DIM), HEAD_DIM)
        lbl = lbl_ref[:, cols]
        lb_e = jnp.exp(lbl - jnp.max(lbl, axis=0, keepdims=True))
        lb = jnp.sum(lb_e[0:layer + 1], axis=0, keepdims=True) / jnp.sum(lb_e, axis=0, keepdims=True)

        fl = _pad_rows(f_ref[:, cols], R)
        qx = _pad_rows(q_ref[:, cols], R)
        vv = _pad_rows(i_ref[:, cols], R)
        log_f = jnp.where(valid, jnp.log(lb + (1.0 - lb) * _sigmoid(fl)), 0.0)
        kk = jnp.where(valid, (1.0 - lb) * _sigmoid(-fl), 0.0)
        qq = qx * _sigmoid(qx)
        bc = _dot_01x(chunk_tri, log_f)
        yield
        q_scr[...] = qq
        k_scr[...] = kk
        v_scr[...] = vv
        b_scr[...] = bc

        def block_row(period, row):
            return jnp.concatenate(
                [jnp.broadcast_to(b_scr[start + row:start + row + 1, :], (period, HEAD_DIM))
                 for start in range(0, R, period)], axis=0)

        b_last = block_row(GLA_CHUNK, GLA_CHUNK - 1)
        qd = qq * jnp.exp(bc)
        kd = kk * jnp.exp(b_last - bc)

        scores = jnp.zeros((R, R), F32)
        hs = GLA_SUB
        while hs < GLA_CHUNK:
            b_mid = block_row(2 * hs, hs - 1)
            lower = (rows & hs) != 0
            q_fac = jnp.where(lower, qq * jnp.exp(jnp.minimum(bc - b_mid, 0.0)), 0.0)
            k_fac = jnp.where(lower, 0.0, kk * jnp.exp(jnp.minimum(b_mid - bc, 0.0)))
            level = _bdot_general(q_fac, k_fac, NT_DIMS)
            yield
            scores = jnp.where((pair_key >= hs) & (pair_key < 2 * hs), level, scores)
            hs *= 2
        o_off = _bdot(scores, vv)
        yield
        return qd, kd, vv, o_off

    def diag_block(m, carry):
        r0 = pl.multiple_of(m * GLA_SUB, GLA_SUB)
        for slot in range(HEAD_GROUP):
            q_scr, k_scr, v_scr, b_scr, od_scr = (r.at[slot] for r in row_scr)
            q_blk = q_scr[pl.ds(r0, GLA_SUB), :]
            b_blk = b_scr[pl.ds(r0, GLA_SUB), :]
            acc = jnp.zeros((GLA_SUB, HEAD_DIM), F32)
            for s in range(GLA_SUB):
                b_s = b_scr[pl.ds(r0 + s, 1), :]
                k_s = k_scr[pl.ds(r0 + s, 1), :]
                v_s = v_scr[pl.ds(r0 + s, 1), :]
                e = jnp.where(row_local >= s, jnp.exp(jnp.minimum(b_blk - b_s, 0.0)), 0.0)
                col = jnp.sum(q_blk * k_s * e, axis=-1, keepdims=True)
                acc = acc + col * v_s
            od_scr[pl.ds(r0, GLA_SUB), :] = acc
        return carry

    def head_post(h, slot, pre, s_t):
        b_scr, od_scr = row_scr[3].at[slot], row_scr[4].at[slot]
        qd, kd, vv, o_off = pre
        o_inter = []
        for c in range(n_chunks):
            sl = slice(c * GLA_CHUNK, (c + 1) * GLA_CHUNK)
            o_inter.append(_bdot_general(qd[sl], s_t, NT_DIMS))
            f_last = jnp.exp(b_scr[(c + 1) * GLA_CHUNK - 1:(c + 1) * GLA_CHUNK, :])
            s_t = s_t * f_last + _bdot_general(vv[sl], kd[sl], TN_DIMS)
            yield

        o = jnp.concatenate(o_inter, axis=0) + o_off + od_scr[...]
        on = o * lax.rsqrt(jnp.mean(o * o, axis=-1, keepdims=True) + EPS) * nb_ref[...]
        cols = pl.ds(pl.multiple_of(h * HEAD_DIM, HEAD_DIM), HEAD_DIM)
        z = z_ref[:, cols]
        o_ref[:, cols] = (on[:tin] * _sigmoid(z)).astype(BF16)
        return s_t

    def head_group(hg, carry):
        heads = [hg * HEAD_GROUP + u for u in range(HEAD_GROUP)]
        states = [s_scr[h] for h in heads]
        pre = _trace_round_robin([head_pre(h, u) for u, h in enumerate(heads)])
        lax.fori_loop(0, R // GLA_SUB, diag_block, 0, unroll=DIAG_UNROLL)
        states = _trace_round_robin([head_post(h, u, p, s) for u, (h, p, s) in enumerate(zip(heads, pre, states))])
        for h, s in zip(heads, states):
            s_scr[h] = s
        return carry

    lax.fori_loop(0, N_HEADS // HEAD_GROUP, head_group, 0)

    @pl.when(t == nt - 1)
    def _():
        sout_ref[0] = s_scr[...]


def _gla(proj, lb_logits, norm_b, s0_t, *, batch, tin, nt, layer):
    rows = _compute_rows(tin)
    body = functools.partial(_gla_body, tin=tin, nt=nt, layer=layer, rows=rows)
    col_spec = lambda c: pl.BlockSpec((tin, D_HEADS), lambda b, t: (b * nt + t, c))
    head_rows = pltpu.VMEM((HEAD_GROUP, rows, HEAD_DIM), F32)
    return pl.pallas_call(
        body,
        grid=(batch, nt),
        in_specs=[
            col_spec(4), col_spec(5), col_spec(6), col_spec(7),
            pl.BlockSpec(lb_logits.shape, lambda b, t: (0, 0)),
            pl.BlockSpec((1, HEAD_DIM), lambda b, t: (0, 0)),
            pl.BlockSpec((1, N_HEADS, HEAD_DIM, HEAD_DIM), lambda b, t: (b, 0, 0, 0)),
        ],
        out_specs=[
            pl.BlockSpec((tin, D_HEADS), lambda b, t: (b * nt + t, 0)),
            pl.BlockSpec((1, N_HEADS, HEAD_DIM, HEAD_DIM), lambda b, t: (b, 0, 0, 0)),
        ],
        out_shape=[jax.ShapeDtypeStruct((batch * nt * tin, D_HEADS), BF16),
                   jax.ShapeDtypeStruct((batch, N_HEADS, HEAD_DIM, HEAD_DIM), F32)],
        scratch_shapes=[pltpu.VMEM((N_HEADS, HEAD_DIM, HEAD_DIM), F32),
                        head_rows, head_rows, head_rows, head_rows, head_rows],
        compiler_params=pltpu.CompilerParams(
            dimension_semantics=("parallel", "arbitrary"), vmem_limit_bytes=VMEM_LIMIT),
        name="hgrn2",
    )(proj, proj, proj, proj, lb_logits, norm_b, s0_t)


def _merge_body(oa_ref, ob_ref, g0_ref, g1_ref, x_ref, wb0_ref, wb1_ref, wo_ref, nf_ref, wr_ref, br_ref,
                x1_ref, h_ref, gate_ref, idx_ref):
    ua = jnp.dot(oa_ref[...], wb0_ref[...], preferred_element_type=F32)
    ub = jnp.dot(ob_ref[...], wb1_ref[...], preferred_element_type=F32)
    merged = _sigmoid(g0_ref[...]) * ua + _sigmoid(g1_ref[...]) * ub
    x1 = x_ref[...] + jnp.dot(merged.astype(BF16), wo_ref[...], preferred_element_type=F32)
    x1_ref[...] = x1
    hn = x1 * lax.rsqrt(jnp.mean(x1 * x1, axis=-1, keepdims=True) + EPS) * nf_ref[...]
    h_ref[...] = hn
    h_hi = hn.astype(BF16)
    h_lo = (hn - h_hi.astype(F32)).astype(BF16)
    logits = jnp.dot(jnp.concatenate([h_hi, h_hi, h_lo], axis=1), wr_ref[...],
                     preferred_element_type=F32) + br_ref[...]

    lane = lax.broadcasted_iota(jnp.int32, logits.shape, 1)
    gates = jnp.zeros(logits.shape, F32)
    picks = jnp.zeros(logits.shape, jnp.int32)
    top = None
    for k in range(TOP_K):
        best = jnp.max(logits, axis=-1, keepdims=True)
        pick = jnp.min(jnp.where(logits == best, lane, HEAD_DIM), axis=-1, keepdims=True)
        top = best if top is None else top
        gates = jnp.where(lane == k, jnp.exp(best - top), gates)
        picks = jnp.where(lane == k, pick, picks)
        logits = jnp.where(lane == pick, ROUTER_PAD, logits)
    gate_ref[...] = gates / jnp.sum(gates, axis=-1, keepdims=True)
    idx_ref[...] = picks


def _merge(o_a, o_b, proj, x, wb0, wb1, wo, norm_ffn, w_router, b_router, tm):
    n, d = x.shape
    const = lambda shape: pl.BlockSpec(shape, lambda i: (0, 0), pipeline_mode=pl.Buffered(1))
    return pl.pallas_call(
        _merge_body,
        grid=(n // tm,),
        in_specs=[
            pl.BlockSpec((tm, D_HEADS), lambda i: (i, 0)),
            pl.BlockSpec((tm, D_HEADS), lambda i: (i, 0)),
            pl.BlockSpec((tm, d), lambda i: (i, 8)),
            pl.BlockSpec((tm, d), lambda i: (i, 9)),
            pl.BlockSpec((tm, d), lambda i: (i, 0)),
            const(wb0.shape), const(wb1.shape), const(wo.shape),
            const((1, d)), const(w_router.shape), const((1, HEAD_DIM)),
        ],
        out_specs=[pl.BlockSpec((tm, d), lambda i: (i, 0)),
                   pl.BlockSpec((tm, d), lambda i: (i, 0)),
                   pl.BlockSpec((tm, HEAD_DIM), lambda i: (i, 0)),
                   pl.BlockSpec((tm, HEAD_DIM), lambda i: (i, 0))],
        out_shape=[jax.ShapeDtypeStruct((n, d), F32),
                   jax.ShapeDtypeStruct((n, d), F32),
                   jax.ShapeDtypeStruct((n, HEAD_DIM), F32),
                   jax.ShapeDtypeStruct((n, HEAD_DIM), jnp.int32)],
        compiler_params=pltpu.CompilerParams(
            dimension_semantics=("parallel",), vmem_limit_bytes=VMEM_LIMIT),
        name="merge_out",
    )(o_a, o_b, proj, proj, x, wb0, wb1, wo, norm_ffn, w_router, b_router)


def _moe_body(be_ref, nu_ref, x_ref, wg_ref, wu_ref, bg_ref, bu_ref, wd_ref, bd_ref, o_ref):
    i = pl.program_id(0)
    f = pl.program_id(1)

    @pl.when(i < nu_ref[0])
    def _():
        x = x_ref[...].astype(BF16)
        gate = jnp.dot(x, wg_ref[0], preferred_element_type=F32) + bg_ref[0]
        up = jnp.dot(x, wu_ref[0], preferred_element_type=F32) + bu_ref[0]
        gate = jnp.minimum(gate, SWIGLU_LIMIT)
        up = jnp.clip(up, -SWIGLU_LIMIT, SWIGLU_LIMIT)
        act = (up + 1.0) * gate * _sigmoid(SWIGLU_ALPHA * gate)
        part = jnp.dot(act.astype(BF16), wd_ref[0], preferred_element_type=F32)

        @pl.when(f == 0)
        def _():
            o_ref[...] = part + bd_ref[0]

        @pl.when(f > 0)
        def _():
            o_ref[...] += part

    @pl.when((i >= nu_ref[0]) & (f == 0))
    def _():
        o_ref[...] = jnp.zeros(o_ref.shape, F32)


def _moe(block_expert, n_used, xs, w_gu, b_gu, w_d, b_d):
    n_rows, d = xs.shape
    n_exp, _, two_ff = w_gu.shape
    d_ff = two_ff // 2
    tf = min(MOE_FF_TILE, d_ff)
    nf = d_ff // tf
    nblk = n_rows // MOE_ROWS
    grid_spec = pltpu.PrefetchScalarGridSpec(
        num_scalar_prefetch=2,
        grid=(nblk, nf),
        in_specs=[
            pl.BlockSpec((MOE_ROWS, d), lambda i, f, be, nu: (i, 0)),
            pl.BlockSpec((1, d, tf), lambda i, f, be, nu: (be[i], 0, f)),
            pl.BlockSpec((1, d, tf), lambda i, f, be, nu: (be[i], 0, nf + f)),
            pl.BlockSpec((1, 1, tf), lambda i, f, be, nu: (be[i], 0, f)),
            pl.BlockSpec((1, 1, tf), lambda i, f, be, nu: (be[i], 0, nf + f)),
            pl.BlockSpec((1, tf, d), lambda i, f, be, nu: (be[i], f, 0)),
            pl.BlockSpec((1, 1, d), lambda i, f, be, nu: (be[i], 0, 0)),
        ],
        out_specs=pl.BlockSpec((MOE_ROWS, d), lambda i, f, be, nu: (i, 0)),
    )
    return pl.pallas_call(
        _moe_body,
        grid_spec=grid_spec,
        out_shape=jax.ShapeDtypeStruct((n_rows, d), F32),
        compiler_params=pltpu.CompilerParams(
            dimension_semantics=("arbitrary", "arbitrary"), vmem_limit_bytes=VMEM_LIMIT),
        name="expert_mlp",
    )(block_expert, n_used, xs, w_gu, w_gu, b_gu, b_gu, w_d, b_d)


def _row_copy(src_ref, src_row, dst_ref, dst_row, sem):
    return pltpu.make_async_copy(src_ref.at[pl.ds(src_row, 1)], dst_ref.at[pl.ds(dst_row, 1)], sem)


def _dispatch_body(dest_ref, fill_ref, *refs, tm, tiles):
    h_refs = refs[:len(tiles)]
    xs_ref, zero_row, sem, zero_sem = refs[len(tiles):]
    i = pl.program_id(0)

    def fill(wait_only):
        def one_range(j, carry):
            def one_row(r, c):
                copy = _row_copy(zero_row, 0, xs_ref, r, zero_sem)
                copy.wait() if wait_only else copy.start()
                return c
            lax.fori_loop(fill_ref[0, j], fill_ref[1, j], one_row, 0)
            return carry
        lax.fori_loop(0, fill_ref.shape[1], one_range, 0)

    @pl.when(i == 0)
    def _():
        zero_row[...] = jnp.zeros(zero_row.shape, F32)
        fill(wait_only=False)
        fill(wait_only=True)

    def scatter(h_ref):
        def start(r, carry):
            for k in range(TOP_K):
                _row_copy(h_ref, r, xs_ref, dest_ref[0, 0, r * TOP_K + k], sem).start()
            return carry

        def wait(r, carry):
            for _ in range(TOP_K):
                _row_copy(h_ref, 0, xs_ref, 0, sem).wait()
            return carry

        lax.fori_loop(0, tm, start, 0, unroll=ROW_COPY_UNROLL)
        lax.fori_loop(0, tm, wait, 0, unroll=ROW_COPY_UNROLL)

    first = 0
    for h_ref, n_tiles in zip(h_refs, tiles):
        pl.when((i >= first) & (i < first + n_tiles))(functools.partial(scatter, h_ref))
        first += n_tiles


def _dispatch(dest, hs, tm, n_rows, fill_rows):
    d = hs[0].shape[1]
    tiles = tuple(h.shape[0] // tm for h in hs)
    starts = [sum(tiles[:g]) for g in range(len(tiles))]
    h_specs = [pl.BlockSpec((tm, d), lambda i, lo=lo, nt=nt: (jnp.clip(i - lo, 0, nt - 1), 0))
               for lo, nt in zip(starts, tiles)]
    return pl.pallas_call(
        functools.partial(_dispatch_body, tm=tm, tiles=tiles),
        grid=(sum(tiles),),
        in_specs=[pl.BlockSpec((1, 1, tm * TOP_K), lambda i: (i, 0, 0), memory_space=pltpu.SMEM),
                  pl.BlockSpec(memory_space=pltpu.SMEM)] + h_specs,
        out_specs=pl.BlockSpec(memory_space=pl.ANY),
        out_shape=jax.ShapeDtypeStruct((n_rows, d), F32),
        scratch_shapes=[pltpu.VMEM((CONV_PAD, d), F32), pltpu.SemaphoreType.DMA(()), pltpu.SemaphoreType.DMA(())],
        compiler_params=pltpu.CompilerParams(
            dimension_semantics=("arbitrary",), vmem_limit_bytes=VMEM_LIMIT),
        name="dispatch_rows",
    )(dest.reshape(sum(tiles), 1, tm * TOP_K), fill_rows, *hs)


def _combine_body(dest_ref, gate_ref, x_ref, g_ref, ys_ref, o_ref, ybuf, sems, *, tm, apply_norm):
    half = tm // 2

    def gather(part, wait_only):
        def one_row(r, carry):
            for k in range(TOP_K):
                if wait_only:
                    _row_copy(ys_ref, 0, ybuf.at[k], 0, sems.at[part]).wait()
                else:
                    _row_copy(ys_ref, dest_ref[0, 0, r * TOP_K + k], ybuf.at[k], r, sems.at[part]).start()
            return carry
        lax.fori_loop(part * half, (part + 1) * half, one_row, 0, unroll=ROW_COPY_UNROLL)

    gather(0, wait_only=False)
    gather(1, wait_only=False)
    for part in range(2):
        gather(part, wait_only=True)
        rows = slice(part * half, (part + 1) * half)
        gate = gate_ref[rows, :]
        y = gate[:, 0:1] * ybuf[0, rows, :]
        for k in range(1, TOP_K):
            y = y + gate[:, k:k + 1] * ybuf[k, rows, :]
        x = x_ref[rows, :] + y
        if apply_norm:
            x = x * lax.rsqrt(jnp.mean(x * x, axis=-1, keepdims=True) + EPS) * g_ref[...]
        o_ref[rows, :] = x


def _combine(dest, gate, x1, gain, ys, tm, *, apply_norm):
    n, d = x1.shape
    return pl.pallas_call(
        functools.partial(_combine_body, tm=tm, apply_norm=apply_norm),
        grid=(n // tm,),
        in_specs=[pl.BlockSpec((1, 1, tm * TOP_K), lambda i: (i, 0, 0), memory_space=pltpu.SMEM),
                  pl.BlockSpec((tm, TOP_K), lambda i: (i, 0)),
                  pl.BlockSpec((tm, d), lambda i: (i, 0)),
                  pl.BlockSpec((1, d), lambda i: (0, 0)),
                  pl.BlockSpec(memory_space=pl.ANY)],
        out_specs=pl.BlockSpec((tm, d), lambda i: (i, 0)),
        out_shape=jax.ShapeDtypeStruct((n, d), F32),
        scratch_shapes=[pltpu.VMEM((TOP_K, tm, d), F32), pltpu.SemaphoreType.DMA((2,))],
        compiler_params=pltpu.CompilerParams(
            dimension_semantics=("arbitrary",), vmem_limit_bytes=VMEM_LIMIT),
        name="combine_final_norm",
    )(dest.reshape(n // tm, 1, tm * TOP_K), gate, x1, gain, ys)


def _token_tile(n, candidates):
    for c in candidates:
        if n % c == 0:
            return c
    raise ValueError(f"token count {n} has no tile in {candidates}")


def _lane_vector(values, first_lane):
    out = jnp.zeros((1, HEAD_DIM), F32)
    return lax.dynamic_update_slice(out, values.reshape(1, -1).astype(F32), (0, first_lane))


def _route(top_idx, n_exp):
    n_tok = top_idx.shape[0]
    n_assign = n_tok * TOP_K
    e_flat = top_idx.reshape(-1)
    onehot = (e_flat[:, None] == jnp.arange(n_exp)[None, :]).astype(jnp.int32)
    seen = jnp.cumsum(onehot, axis=0)
    rank = jnp.sum(onehot * seen, axis=1) - 1
    counts = seen[-1]
    padded = ((counts + MOE_ROWS - 1) // MOE_ROWS) * MOE_ROWS
    pend = jnp.cumsum(padded)
    pstart = pend - padded
    dest = (pstart[e_flat] + rank).astype(jnp.int32).reshape(n_tok, TOP_K)
    n_blocks = -(-n_assign // MOE_ROWS) + n_exp
    block_start = jnp.arange(n_blocks, dtype=pend.dtype) * MOE_ROWS
    block_expert = jnp.minimum(
        jnp.sum(pend[None, :] <= block_start[:, None], axis=1), n_exp - 1).astype(jnp.int32)
    n_used = (pend[-1] // MOE_ROWS).astype(jnp.int32).reshape(1)
    n_rows = n_blocks * MOE_ROWS
    fill_rows = jnp.stack([jnp.append(pstart + counts, pend[-1]), jnp.append(pend, n_rows)]).astype(jnp.int32)
    return dest, fill_rows, block_expert, n_used, n_rows


def kernel(x_prompt, x_sample, state_conv_a, state_delta, state_hgrn, norm_mix, w_in, conv_a, a_log, dt_bias,
           norm_a, lb_logits, norm_b, w_branch, w_out, norm_ffn, w_router, b_router, w_gate_up, b_gate_up,
           w_down, b_down, norm_final):
    bp, tp, d = x_prompt.shape
    bs, ts, _ = x_sample.shape
    depth = w_in.shape[0]
    n_exp = w_router.shape[-1]
    assert d == D_HEADS and tp % BLOCK_ROWS == 0 and ts <= BLOCK_ROWS
    assert min(tp, ts) >= CONV_WIDTH - 1
    d_conv = 3 * D_HEADS
    tail = CONV_WIDTH - 1

    groups = [
        dict(x=x_prompt.reshape(bp * tp, d), batch=bp, t=tp, tin=BLOCK_ROWS, first=True),
        dict(x=x_sample.reshape(bs * ts, d), batch=bs, t=ts, tin=ts, first=False),
    ]
    for g in groups:
        g['n'] = g['batch'] * g['t']
        g['tm_in'] = _token_tile(g['n'], (1024, 768, 512, 256))
        g['tm'] = _token_tile(g['n'], (256,))
        g['conv'], g['delta'], g['hgrn'] = [], [], []

    for l in range(depth):
        w_l = w_in[l]
        w_main = jnp.concatenate([w_l[:, :d_conv], w_l[:, d_conv + 2 * N_HEADS:]], axis=1).astype(BF16)
        w_ba = jnp.pad(w_l[:, d_conv:d_conv + 2 * N_HEADS], ((0, 0), (0, HEAD_DIM - 2 * N_HEADS))).astype(BF16)
        gain = norm_mix[l].reshape(1, d)
        conv_w = jnp.pad(conv_a[l], ((0, CONV_PAD - CONV_WIDTH), (0, 0)))
        alog_l = _lane_vector(a_log[l], N_HEADS)
        dtb_l = _lane_vector(dt_bias[l], N_HEADS)
        na = norm_a[l].reshape(1, HEAD_DIM)
        nb = norm_b[l].reshape(1, HEAD_DIM)
        lbl = jnp.pad(lb_logits.astype(F32), ((0, CONV_PAD - lb_logits.shape[0]), (0, 0)), constant_values=-1e30)
        w_r = jnp.pad(w_router[l], ((0, 0), (0, HEAD_DIM - n_exp)))
        w_r_hi = w_r.astype(BF16)
        w_r = jnp.concatenate([w_r_hi, (w_r - w_r_hi.astype(F32)).astype(BF16), w_r_hi], axis=0)
        b_r = jnp.pad(b_router[l], (0, HEAD_DIM - n_exp), constant_values=ROUTER_PAD).reshape(1, HEAD_DIM)
        wb0, wb1, wo = w_branch[l, 0].astype(BF16), w_branch[l, 1].astype(BF16), w_out[l].astype(BF16)

        for g in groups:
            batch, t, tin = g['batch'], g['t'], g['tin']
            nt = t // tin
            proj = _inproj(g['x'], gain, w_main, g['tm_in'], D_HEADS)
            ba = _inproj(g['x'], gain, w_ba, g['tm_in'], HEAD_DIM)
            if g['first']:
                cs = jnp.zeros((batch, CONV_PAD, d_conv), F32)
                s_delta = jnp.zeros((batch, N_HEADS, HEAD_DIM, HEAD_DIM), F32)
                s_hgrn_t = s_delta
            else:
                cs = jnp.pad(state_conv_a[l], ((0, 0), (CONV_PAD - tail, 0), (0, 0)))
                s_delta = state_delta[l]
                s_hgrn_t = jnp.swapaxes(state_hgrn[l], -1, -2)
            o_a, sd = _gdn(proj, ba, cs, s_delta, conv_w, alog_l, dtb_l, na, batch=batch, tin=tin, nt=nt)
            o_b, sh = _gla(proj, lbl, nb, s_hgrn_t, batch=batch, tin=tin, nt=nt, layer=l)
            g['conv'].append(jnp.stack([proj[(b + 1) * t - tail:(b + 1) * t, :d_conv] for b in range(batch)]))
            g['delta'].append(sd)
            g['hgrn'].append(jnp.swapaxes(sh, -1, -2))
            g['x1'], g['h'], g['gates'], g['top_idx'] = _merge(o_a, o_b, proj, g['x'], wb0, wb1, wo,
                                                  norm_ffn[l].reshape(1, d), w_r, b_r, g['tm'])

        top_idx = jnp.concatenate([g['top_idx'][:, :TOP_K] for g in groups], axis=0)
        dest, fill_rows, block_expert, n_used, n_rows = _route(top_idx, n_exp)
        row = 0
        for g in groups:
            g['dest'] = dest[row:row + g['n']]
            g['gate'] = g['gates'][:, :TOP_K]
            row += g['n']
        tm_rows = groups[0]['tm']
        assert all(g['tm'] == tm_rows for g in groups)
        xs = _dispatch(dest, [g['h'] for g in groups], tm_rows, n_rows, fill_rows)
        ys = _moe(block_expert, n_used, xs, w_gate_up[l].astype(BF16), b_gate_up[l][:, None, :],
                  w_down[l].astype(BF16), b_down[l][:, None, :])
        for g in groups:
            g['x'] = _combine(g['dest'], g['gate'], g['x1'], norm_final.reshape(1, d), ys, g['tm'],
                              apply_norm=l + 1 == depth)

    gp, gs = groups
    return (gp['x'].reshape(bp, tp, d), gs['x'].reshape(bs, ts, d),
            jnp.stack(gp['conv']), jnp.stack(gp['delta']), jnp.stack(gp['hgrn']),
            jnp.stack(gs['conv']), jnp.stack(gs['delta']), jnp.stack(gs['hgrn']))
```

```python
import functools

import jax
import jax.numpy as jnp
from jax import lax
from jax.experimental import pallas as pl
from jax.experimental.pallas import tpu as pltpu

F32 = jnp.float32
BF16 = jnp.bfloat16

EPS = 1e-6
HEAD_DIM = 128
N_HEADS = 16
D_HEADS = N_HEADS * HEAD_DIM
CONV_WIDTH = 4
CONV_PAD = 8
BLOCK_ROWS = 256
GLA_CHUNK = 64
GLA_SUB = 8
HEAD_GROUP = 2
GDN_HEAD_GROUP = 4
DIAG_UNROLL = True
TOP_K = 4
ROUTER_PAD = -1e30
SWIGLU_LIMIT = 7.0
SWIGLU_ALPHA = 1.702
MOE_ROWS = 512
MOE_FF_TILE = 1024
ROW_COPY_UNROLL = 2
VMEM_LIMIT = 56 * 1024 * 1024

NT_DIMS = (((1,), (1,)), ((), ()))
TN_DIMS = (((0,), (0,)), ((), ()))


def _sigmoid(x):
    return 1.0 / (1.0 + jnp.exp(-x))


def _bdot(a, b):
    return jnp.dot(a.astype(BF16), b.astype(BF16), preferred_element_type=F32)


def _bdot_general(a, b, dims):
    return lax.dot_general(a.astype(BF16), b.astype(BF16), dims, preferred_element_type=F32)


def _split3(x):
    hi = x.astype(BF16)
    r1 = x - hi.astype(F32)
    mid = r1.astype(BF16)
    lo = (r1 - mid.astype(F32)).astype(BF16)
    return hi, mid, lo


def _dot_x01(x, m01):
    m = m01.astype(BF16)
    return jnp.dot(jnp.concatenate(_split3(x), axis=1), jnp.concatenate([m, m, m], axis=0),
                   preferred_element_type=F32)


def _dot_01x(m01, x):
    m = m01.astype(BF16)
    return jnp.dot(jnp.concatenate([m, m, m], axis=1), jnp.concatenate(_split3(x), axis=0),
                   preferred_element_type=F32)


def _dot_01xt(m01, x):
    m = m01.astype(BF16)
    return lax.dot_general(jnp.concatenate([m, m, m], axis=1), jnp.concatenate(_split3(x), axis=1), NT_DIMS,
                           preferred_element_type=F32)


def _trace_round_robin(gens):
    results = [None] * len(gens)
    live = list(range(len(gens)))
    while live:
        for i in list(live):
            try:
                next(gens[i])
            except StopIteration as stop:
                results[i] = stop.value
                live.remove(i)
    return results


def _inproj_body(x_ref, g_ref, w_ref, o_ref, xn_ref):
    @pl.when(pl.program_id(1) == 0)
    def _():
        x = x_ref[...]
        ms = jnp.mean(x * x, axis=-1, keepdims=True)
        xn_ref[...] = (x * lax.rsqrt(ms + EPS) * g_ref[...]).astype(BF16)

    o_ref[...] = jnp.dot(xn_ref[...], w_ref[...], preferred_element_type=F32)


def _inproj(x, gain, w, tm, tn):
    n, d = x.shape
    nc = w.shape[1]
    return pl.pallas_call(
        _inproj_body,
        grid=(n // tm, nc // tn),
        in_specs=[pl.BlockSpec((tm, d), lambda i, j: (i, 0)),
                  pl.BlockSpec((1, d), lambda i, j: (0, 0)),
                  pl.BlockSpec((d, tn), lambda i, j: (0, j))],
        out_specs=pl.BlockSpec((tm, tn), lambda i, j: (i, j)),
        out_shape=jax.ShapeDtypeStruct((n, nc), F32),
        scratch_shapes=[pltpu.VMEM((tm, d), BF16)],
        compiler_params=pltpu.CompilerParams(
            dimension_semantics=("parallel", "arbitrary"), vmem_limit_bytes=VMEM_LIMIT),
        name="inproj",
    )(x, gain, w)


def _compute_rows(tin):
    for rows in (HEAD_DIM, BLOCK_ROWS):
        if tin <= rows:
            return rows
    raise ValueError(f"block of {tin} tokens exceeds {BLOCK_ROWS}")


def _pad_rows(x, rows):
    if x.shape[0] == rows:
        return x
    return jnp.concatenate([x, jnp.zeros((rows - x.shape[0],) + x.shape[1:], x.dtype)], axis=0)


def _gdn_body(qkv_ref, z_ref, ba_ref, cs_ref, s0_ref, cw_ref, alog_ref, dtb_ref, na_ref,
              o_ref, sout_ref, xbuf, s_scr, *, tin, nt, rows):
    R = rows
    t = pl.program_id(1)

    @pl.when(t == 0)
    def _():
        xbuf[0:CONV_PAD, :] = cs_ref[0]
        s_scr[...] = s0_ref[0]

    if nt > 1:
        @pl.when(t > 0)
        def _():
            xbuf[0:CONV_PAD, :] = xbuf[tin:tin + CONV_PAD, :]

    xbuf[CONV_PAD:CONV_PAD + tin, :] = qkv_ref[...]
    if tin < R:
        xbuf[CONV_PAD + tin:CONV_PAD + R, :] = jnp.zeros((R - tin, xbuf.shape[1]), F32)

    rows = lax.broadcasted_iota(jnp.int32, (R, HEAD_DIM), 0)
    lanes = lax.broadcasted_iota(jnp.int32, (R, HEAD_DIM), 1)
    valid = rows < tin
    ti = lax.broadcasted_iota(jnp.int32, (R, R), 0)
    si = lax.broadcasted_iota(jnp.int32, (R, R), 1)
    causal = ti >= si
    merge_key = jnp.where(ti > si, ti ^ si, 0)

    ba = _pad_rows(ba_ref[...], R)
    sp_in = ba + dtb_ref[...]
    softplus = jnp.maximum(sp_in, 0.0) + jnp.log(1.0 + jnp.exp(-jnp.abs(sp_in)))
    g_all = jnp.where(valid, -jnp.exp(alog_ref[...]) * softplus, 0.0)
    beta_all = jnp.where(valid, _sigmoid(ba), 0.0)
    gcum_all = _dot_01x(causal, g_all)
    gate_src = jnp.where(lanes < N_HEADS, beta_all, gcum_all)
    DIAG = HEAD_DIM
    assert R in (DIAG, 2 * DIAG)
    diag_key = merge_key[:DIAG, :DIAG]

    sel_l = lax.broadcasted_iota(jnp.int32, (HEAD_DIM, 2 * HEAD_DIM), 0)
    sel_c = lax.broadcasted_iota(jnp.int32, (HEAD_DIM, 2 * HEAD_DIM), 1)
    lane0 = (lanes == 0).astype(F32)

    def head(h, s_old):
        off = h * HEAD_DIM

        def conv_silu(base):
            cols = pl.ds(pl.multiple_of(base + off, HEAD_DIM), HEAD_DIM)
            acc = None
            for j in range(CONV_WIDTH):
                term = xbuf[pl.ds(CONV_PAD - (CONV_WIDTH - 1) + j, R), cols] * cw_ref[j:j + 1, cols]
                acc = term if acc is None else acc + term
            return acc * _sigmoid(acc)

        qc = conv_silu(0)
        kc = conv_silu(D_HEADS)
        vc = conv_silu(2 * D_HEADS)
        qn = qc * lax.rsqrt(jnp.sum(qc * qc, axis=-1, keepdims=True) + EPS) * (HEAD_DIM ** -0.5)
        kn = jnp.where(valid, kc * lax.rsqrt(jnp.sum(kc * kc, axis=-1, keepdims=True) + EPS), 0.0)

        sel = (((sel_c < HEAD_DIM) & (sel_l == h)) |
               ((sel_c >= HEAD_DIM) & (sel_l == h + N_HEADS))).astype(F32)
        bg = _dot_x01(gate_src, sel)
        yield
        beta = bg[:, :HEAD_DIM]
        gc = bg[:, HEAD_DIM:]
        gc_last = gc[R - 1:R, :]
        gc_t = jnp.concatenate([gc] * (R // HEAD_DIM), axis=1)
        gc_s = _dot_01xt(lane0, gc)
        yield
        decay = jnp.where(causal, jnp.exp(jnp.where(causal, gc_t - gc_s, 0.0)), 0.0)

        kb = kn * beta
        aq = _bdot_general(jnp.concatenate([kb, qn], axis=0), kn, NT_DIMS)
        yield
        a_mat = jnp.where(ti > si, aq[:R] * decay, 0.0)
        qk = aq[R:] * decay

        a_diag = [a_mat[i * DIAG:(i + 1) * DIAG, i * DIAG:(i + 1) * DIAG] for i in range(R // DIAG)]
        n_diag = None
        b = 1
        while b < DIAG:
            level = (diag_key >= b) & (diag_key < 2 * b)
            c_lvl = [jnp.where(level, a, 0.0) for a in a_diag]
            if n_diag is None:
                n_diag = [-c for c in c_lvl]
            else:
                p_lvl = [c + _bdot(n, c) for n, c in zip(n_diag, c_lvl)]
                yield
                n_diag = [n - p - _bdot(p, n) for n, p in zip(n_diag, p_lvl)]
                yield
            b *= 2

        rhs = jnp.concatenate([vc * beta, kb * jnp.exp(gc)], axis=1)
        if len(a_diag) == 1:
            sol = rhs + _bdot(n_diag[0], rhs)
        else:
            n_top, n_bot = n_diag
            a_cross = a_mat[DIAG:, :DIAG]
            p_cross = a_cross + _bdot(n_bot, a_cross)
            yield
            n_cross = -(p_cross + _bdot(p_cross, n_top))
            yield
            rhs_top, rhs_bot = rhs[:DIAG], rhs[DIAG:]
            sol = jnp.concatenate([rhs_top + _bdot(n_top, rhs_top),
                                   rhs_bot + _bdot(n_cross, rhs_top) + _bdot(n_bot, rhs_bot)], axis=0)
        yield
        u = sol[:, :HEAD_DIM]
        w = sol[:, HEAD_DIM:]

        qd = qn * jnp.exp(gc)
        kd = kn * jnp.exp(gc_last - gc)
        ws = _bdot(jnp.concatenate([w, qd], axis=0), s_old)
        yield
        v_new = u - ws[:R]
        o = ws[R:] + _bdot(qk, v_new)
        s_new = s_old * jnp.exp(gc_last) + _bdot_general(kd, v_new, TN_DIMS)
        yield

        cols = pl.ds(pl.multiple_of(off, HEAD_DIM), HEAD_DIM)
        on = o * lax.rsqrt(jnp.mean(o * o, axis=-1, keepdims=True) + EPS) * na_ref[...]
        z = z_ref[:, cols]
        o_ref[:, cols] = (on[:tin] * (z * _sigmoid(z))).astype(BF16)
        return s_new

    def head_group(hg, carry):
        heads = [hg * GDN_HEAD_GROUP + u for u in range(GDN_HEAD_GROUP)]
        states = _trace_round_robin([head(h, s_scr[h]) for h in heads])
        for h, s in zip(heads, states):
            s_scr[h] = s
        return carry

    lax.fori_loop(0, N_HEADS // GDN_HEAD_GROUP, head_group, 0)

    @pl.when(t == nt - 1)
    def _():
        sout_ref[0] = s_scr[...]


def _gdn(proj, ba, conv_state, s0, conv_w, alog_l, dtb_l, norm_a, *, batch, tin, nt):
    rows = _compute_rows(tin)
    body = functools.partial(_gdn_body, tin=tin, nt=nt, rows=rows)
    return pl.pallas_call(
        body,
        grid=(batch, nt),
        in_specs=[
            pl.BlockSpec((tin, 3 * D_HEADS), lambda b, t: (b * nt + t, 0)),
            pl.BlockSpec((tin, D_HEADS), lambda b, t: (b * nt + t, 3)),
            pl.BlockSpec((tin, HEAD_DIM), lambda b, t: (b * nt + t, 0)),
            pl.BlockSpec((1, CONV_PAD, 3 * D_HEADS), lambda b, t: (b, 0, 0)),
            pl.BlockSpec((1, N_HEADS, HEAD_DIM, HEAD_DIM), lambda b, t: (b, 0, 0, 0)),
            pl.BlockSpec((CONV_PAD, 3 * D_HEADS), lambda b, t: (0, 0)),
            pl.BlockSpec((1, HEAD_DIM), lambda b, t: (0, 0)),
            pl.BlockSpec((1, HEAD_DIM), lambda b, t: (0, 0)),
            pl.BlockSpec((1, HEAD_DIM), lambda b, t: (0, 0)),
        ],
        out_specs=[
            pl.BlockSpec((tin, D_HEADS), lambda b, t: (b * nt + t, 0)),
            pl.BlockSpec((1, N_HEADS, HEAD_DIM, HEAD_DIM), lambda b, t: (b, 0, 0, 0)),
        ],
        out_shape=[jax.ShapeDtypeStruct((batch * nt * tin, D_HEADS), BF16),
                   jax.ShapeDtypeStruct((batch, N_HEADS, HEAD_DIM, HEAD_DIM), F32)],
        scratch_shapes=[pltpu.VMEM((rows + CONV_PAD, 3 * D_HEADS), F32),
                        pltpu.VMEM((N_HEADS, HEAD_DIM, HEAD_DIM), F32)],
        compiler_params=pltpu.CompilerParams(
            dimension_semantics=("parallel", "arbitrary"), vmem_limit_bytes=VMEM_LIMIT),
        name="gated_delta",
    )(proj, proj, ba, conv_state, s0, conv_w, alog_l, dtb_l, norm_a)


def _gla_body(q_ref, f_ref, i_ref, z_ref, lbl_ref, nb_ref, s0_ref, o_ref, sout_ref,
              s_scr, *row_scr, tin, nt, layer, rows):
    R = rows
    n_chunks = R // GLA_CHUNK
    t = pl.program_id(1)

    @pl.when(t == 0)
    def _():
        s_scr[...] = s0_ref[0]

    rows = lax.broadcasted_iota(jnp.int32, (R, HEAD_DIM), 0)
    valid = rows < tin
    ti = lax.broadcasted_iota(jnp.int32, (R, R), 0)
    si = lax.broadcasted_iota(jnp.int32, (R, R), 1)
    chunk_tri = ((ti >= si) & ((ti // GLA_CHUNK) == (si // GLA_CHUNK))).astype(F32)
    pair_key = jnp.where(ti > si, ti ^ si, 0)
    row_local = lax.broadcasted_iota(jnp.int32, (GLA_SUB, HEAD_DIM), 0)

    def head_pre(h, slot):
        q_scr, k_scr, v_scr, b_scr, _ = (r.at[slot] for r in row_scr)
        cols = pl.ds(pl.multiple_of(h * HEAD_DIM, HEAD_DIM), HEAD_DIM)
        lbl = lbl_ref[:, cols]
        lb_e = jnp.exp(lbl - jnp.max(lbl, axis=0, keepdims=True))
        lb = jnp.sum(lb_e[0:layer + 1], axis=0, keepdims=True) / jnp.sum(lb_e, axis=0, keepdims=True)

        fl = _pad_rows(f_ref[:, cols], R)
        qx = _pad_rows(q_ref[:, cols], R)
        vv = _pad_rows(i_ref[:, cols], R)
        log_f = jnp.where(valid, jnp.log(lb + (1.0 - lb) * _sigmoid(fl)), 0.0)
        kk = jnp.where(valid, (1.0 - lb) * _sigmoid(-fl), 0.0)
        qq = qx * _sigmoid(qx)
        bc = _dot_01x(chunk_tri, log_f)
        yield
        q_scr[...] = qq
        k_scr[...] = kk
        v_scr[...] = vv
        b_scr[...] = bc

        def block_row(period, row):
            return jnp.concatenate(
                [jnp.broadcast_to(b_scr[start + row:start + row + 1, :], (period, HEAD_DIM))
                 for start in range(0, R, period)], axis=0)

        b_last = block_row(GLA_CHUNK, GLA_CHUNK - 1)
        qd = qq * jnp.exp(bc)
        kd = kk * jnp.exp(b_last - bc)

        scores = jnp.zeros((R, R), F32)
        hs = GLA_SUB
        while hs < GLA_CHUNK:
            b_mid = block_row(2 * hs, hs - 1)
            lower = (rows & hs) != 0
            q_fac = jnp.where(lower, qq * jnp.exp(jnp.minimum(bc - b_mid, 0.0)), 0.0)
            k_fac = jnp.where(lower, 0.0, kk * jnp.exp(jnp.minimum(b_mid - bc, 0.0)))
            level = _bdot_general(q_fac, k_fac, NT_DIMS)
            yield
            scores = jnp.where((pair_key >= hs) & (pair_key < 2 * hs), level, scores)
            hs *= 2
        o_off = _bdot(scores, vv)
        yield
        return qd, kd, vv, o_off

    def diag_block(m, carry):
        r0 = pl.multiple_of(m * GLA_SUB, GLA_SUB)
        for slot in range(HEAD_GROUP):
            q_scr, k_scr, v_scr, b_scr, od_scr = (r.at[slot] for r in row_scr)
            q_blk = q_scr[pl.ds(r0, GLA_SUB), :]
            b_blk = b_scr[pl.ds(r0, GLA_SUB), :]
            acc = jnp.zeros((GLA_SUB, HEAD_DIM), F32)
            for s in range(GLA_SUB):
                b_s = b_scr[pl.ds(r0 + s, 1), :]
                k_s = k_scr[pl.ds(r0 + s, 1), :]
                v_s = v_scr[pl.ds(r0 + s, 1), :]
                e = jnp.where(row_local >= s, jnp.exp(jnp.minimum(b_blk - b_s, 0.0)), 0.0)
                col = jnp.sum(q_blk * k_s * e, axis=-1, keepdims=True)
                acc = acc + col * v_s
            od_scr[pl.ds(r0, GLA_SUB), :] = acc
        return carry

    def head_post(h, slot, pre, s_t):
        b_scr, od_scr = row_scr[3].at[slot], row_scr[4].at[slot]
        qd, kd, vv, o_off = pre
        o_inter = []
        for c in range(n_chunks):
            sl = slice(c * GLA_CHUNK, (c + 1) * GLA_CHUNK)
            o_inter.append(_bdot_general(qd[sl], s_t, NT_DIMS))
            f_last = jnp.exp(b_scr[(c + 1) * GLA_CHUNK - 1:(c + 1) * GLA_CHUNK, :])
            s_t = s_t * f_last + _bdot_general(vv[sl], kd[sl], TN_DIMS)
            yield

        o = jnp.concatenate(o_inter, axis=0) + o_off + od_scr[...]
        on = o * lax.rsqrt(jnp.mean(o * o, axis=-1, keepdims=True) + EPS) * nb_ref[...]
        cols = pl.ds(pl.multiple_of(h * HEAD_DIM, HEAD_DIM), HEAD_DIM)
        z = z_ref[:, cols]
        o_ref[:, cols] = (on[:tin] * _sigmoid(z)).astype(BF16)
        return s_t

    def head_group(hg, carry):
        heads = [hg * HEAD_GROUP + u for u in range(HEAD_GROUP)]
        states = [s_scr[h] for h in heads]
        pre = _trace_round_robin([head_pre(h, u) for u, h in enumerate(heads)])
        lax.fori_loop(0, R // GLA_SUB, diag_block, 0, unroll=DIAG_UNROLL)
        states = _trace_round_robin([head_post(h, u, p, s) for u, (h, p, s) in enumerate(zip(heads, pre, states))])
        for h, s in zip(heads, states):
            s_scr[h] = s
        return carry

    lax.fori_loop(0, N_HEADS // HEAD_GROUP, head_group, 0)

    @pl.when(t == nt - 1)
    def _():
        sout_ref[0] = s_scr[...]


def _gla(proj, lb_logits, norm_b, s0_t, *, batch, tin, nt, layer):
    rows = _compute_rows(tin)
    body = functools.partial(_gla_body, tin=tin, nt=nt, layer=layer, rows=rows)
    col_spec = lambda c: pl.BlockSpec((tin, D_HEADS), lambda b, t: (b * nt + t, c))
    head_rows = pltpu.VMEM((HEAD_GROUP, rows, HEAD_DIM), F32)
    return pl.pallas_call(
        body,
        grid=(batch, nt),
        in_specs=[
            col_spec(4), col_spec(5), col_spec(6), col_spec(7),
            pl.BlockSpec(lb_logits.shape, lambda b, t: (0, 0)),
            pl.BlockSpec((1, HEAD_DIM), lambda b, t: (0, 0)),
            pl.BlockSpec((1, N_HEADS, HEAD_DIM, HEAD_DIM), lambda b, t: (b, 0, 0, 0)),
        ],
        out_specs=[
            pl.BlockSpec((tin, D_HEADS), lambda b, t: (b * nt + t, 0)),
            pl.BlockSpec((1, N_HEADS, HEAD_DIM, HEAD_DIM), lambda b, t: (b, 0, 0, 0)),
        ],
        out_shape=[jax.ShapeDtypeStruct((batch * nt * tin, D_HEADS), BF16),
                   jax.ShapeDtypeStruct((batch, N_HEADS, HEAD_DIM, HEAD_DIM), F32)],
        scratch_shapes=[pltpu.VMEM((N_HEADS, HEAD_DIM, HEAD_DIM), F32),
                        head_rows, head_rows, head_rows, head_rows, head_rows],
        compiler_params=pltpu.CompilerParams(
            dimension_semantics=("parallel", "arbitrary"), vmem_limit_bytes=VMEM_LIMIT),
        name="hgrn2",
    )(proj, proj, proj, proj, lb_logits, norm_b, s0_t)


def _merge_body(oa_ref, ob_ref, g0_ref, g1_ref, x_ref, wb0_ref, wb1_ref, wo_ref, nf_ref, wr_ref, br_ref,
                x1_ref, h_ref, gate_ref, idx_ref):
    ua = jnp.dot(oa_ref[...], wb0_ref[...], preferred_element_type=F32)
    ub = jnp.dot(ob_ref[...], wb1_ref[...], preferred_element_type=F32)
    merged = _sigmoid(g0_ref[...]) * ua + _sigmoid(g1_ref[...]) * ub
    x1 = x_ref[...] + jnp.dot(merged.astype(BF16), wo_ref[...], preferred_element_type=F32)
    x1_ref[...] = x1
    hn = x1 * lax.rsqrt(jnp.mean(x1 * x1, axis=-1, keepdims=True) + EPS) * nf_ref[...]
    h_ref[...] = hn
    h_hi = hn.astype(BF16)
    h_lo = (hn - h_hi.astype(F32)).astype(BF16)
    logits = jnp.dot(jnp.concatenate([h_hi, h_hi, h_lo], axis=1), wr_ref[...],
                     preferred_element_type=F32) + br_ref[...]

    lane = lax.broadcasted_iota(jnp.int32, logits.shape, 1)
    gates = jnp.zeros(logits.shape, F32)
    picks = jnp.zeros(logits.shape, jnp.int32)
    top = None
    for k in range(TOP_K):
        best = jnp.max(logits, axis=-1, keepdims=True)
        pick = jnp.min(jnp.where(logits == best, lane, HEAD_DIM), axis=-1, keepdims=True)
        top = best if top is None else top
        gates = jnp.where(lane == k, jnp.exp(best - top), gates)
        picks = jnp.where(lane == k, pick, picks)
        logits = jnp.where(lane == pick, ROUTER_PAD, logits)
    gate_ref[...] = gates / jnp.sum(gates, axis=-1, keepdims=True)
    idx_ref[...] = picks


def _merge(o_a, o_b, proj, x, wb0, wb1, wo, norm_ffn, w_router, b_router, tm):
    n, d = x.shape
    const = lambda shape: pl.BlockSpec(shape, lambda i: (0, 0), pipeline_mode=pl.Buffered(1))
    return pl.pallas_call(
        _merge_body,
        grid=(n // tm,),
        in_specs=[
            pl.BlockSpec((tm, D_HEADS), lambda i: (i, 0)),
            pl.BlockSpec((tm, D_HEADS), lambda i: (i, 0)),
            pl.BlockSpec((tm, d), lambda i: (i, 8)),
            pl.BlockSpec((tm, d), lambda i: (i, 9)),
            pl.BlockSpec((tm, d), lambda i: (i, 0)),
            const(wb0.shape), const(wb1.shape), const(wo.shape),
            const((1, d)), const(w_router.shape), const((1, HEAD_DIM)),
        ],
        out_specs=[pl.BlockSpec((tm, d), lambda i: (i, 0)),
                   pl.BlockSpec((tm, d), lambda i: (i, 0)),
                   pl.BlockSpec((tm, HEAD_DIM), lambda i: (i, 0)),
                   pl.BlockSpec((tm, HEAD_DIM), lambda i: (i, 0))],
        out_shape=[jax.ShapeDtypeStruct((n, d), F32),
                   jax.ShapeDtypeStruct((n, d), F32),
                   jax.ShapeDtypeStruct((n, HEAD_DIM), F32),
                   jax.ShapeDtypeStruct((n, HEAD_DIM), jnp.int32)],
        compiler_params=pltpu.CompilerParams(
            dimension_semantics=("parallel",), vmem_limit_bytes=VMEM_LIMIT),
        name="merge_out",
    )(o_a, o_b, proj, proj, x, wb0, wb1, wo, norm_ffn, w_router, b_router)


def _moe_body(be_ref, nu_ref, x_ref, wg_ref, wu_ref, bg_ref, bu_ref, wd_ref, bd_ref, o_ref):
    i = pl.program_id(0)
    f = pl.program_id(1)

    @pl.when(i < nu_ref[0])
    def _():
        x = x_ref[...].astype(BF16)
        gate = jnp.dot(x, wg_ref[0], preferred_element_type=F32) + bg_ref[0]
        up = jnp.dot(x, wu_ref[0], preferred_element_type=F32) + bu_ref[0]
        gate = jnp.minimum(gate, SWIGLU_LIMIT)
        up = jnp.clip(up, -SWIGLU_LIMIT, SWIGLU_LIMIT)
        act = (up + 1.0) * gate * _sigmoid(SWIGLU_ALPHA * gate)
        part = jnp.dot(act.astype(BF16), wd_ref[0], preferred_element_type=F32)

        @pl.when(f == 0)
        def _():
            o_ref[...] = part + bd_ref[0]

        @pl.when(f > 0)
        def _():
            o_ref[...] += part

    @pl.when((i >= nu_ref[0]) & (f == 0))
    def _():
        o_ref[...] = jnp.zeros(o_ref.shape, F32)


def _moe(block_expert, n_used, xs, w_gu, b_gu, w_d, b_d):
    n_rows, d = xs.shape
    n_exp, _, two_ff = w_gu.shape
    d_ff = two_ff // 2
    tf = min(MOE_FF_TILE, d_ff)
    nf = d_ff // tf
    nblk = n_rows // MOE_ROWS
    grid_spec = pltpu.PrefetchScalarGridSpec(
        num_scalar_prefetch=2,
        grid=(nblk, nf),
        in_specs=[
            pl.BlockSpec((MOE_ROWS, d), lambda i, f, be, nu: (i, 0)),
            pl.BlockSpec((1, d, tf), lambda i, f, be, nu: (be[i], 0, f)),
            pl.BlockSpec((1, d, tf), lambda i, f, be, nu: (be[i], 0, nf + f)),
            pl.BlockSpec((1, 1, tf), lambda i, f, be, nu: (be[i], 0, f)),
            pl.BlockSpec((1, 1, tf), lambda i, f, be, nu: (be[i], 0, nf + f)),
            pl.BlockSpec((1, tf, d), lambda i, f, be, nu: (be[i], f, 0)),
            pl.BlockSpec((1, 1, d), lambda i, f, be, nu: (be[i], 0, 0)),
        ],
        out_specs=pl.BlockSpec((MOE_ROWS, d), lambda i, f, be, nu: (i, 0)),
    )
    return pl.pallas_call(
        _moe_body,
        grid_spec=grid_spec,
        out_shape=jax.ShapeDtypeStruct((n_rows, d), F32),
        compiler_params=pltpu.CompilerParams(
            dimension_semantics=("arbitrary", "arbitrary"), vmem_limit_bytes=VMEM_LIMIT),
        name="expert_mlp",
    )(block_expert, n_used, xs, w_gu, w_gu, b_gu, b_gu, w_d, b_d)


def _row_copy(src_ref, src_row, dst_ref, dst_row, sem):
    return pltpu.make_async_copy(src_ref.at[pl.ds(src_row, 1)], dst_ref.at[pl.ds(dst_row, 1)], sem)


def _dispatch_body(dest_ref, fill_ref, *refs, tm, tiles):
    h_refs = refs[:len(tiles)]
    xs_ref, zeros, sem, zero_sem = refs[len(tiles):]
    i = pl.program_id(0)
    n_exp = fill_ref.shape[1] - 1

    def fill(wait_only):
        def act(copy):
            copy.wait() if wait_only else copy.start()

        def one_expert(e, carry):
            def one_row(r, c):
                act(_row_copy(zeros, 0, xs_ref, r, zero_sem))
                return c
            lax.fori_loop(fill_ref[0, e], fill_ref[1, e], one_row, 0)
            return carry

        def one_block(j, carry):
            rows = pl.ds(pl.multiple_of(j * MOE_ROWS, MOE_ROWS), MOE_ROWS)
            act(pltpu.make_async_copy(zeros, xs_ref.at[rows], sem))
            return carry

        lax.fori_loop(0, n_exp, one_expert, 0)
        lax.fori_loop(fill_ref[0, n_exp] // MOE_ROWS, fill_ref[1, n_exp] // MOE_ROWS, one_block, 0)

    @pl.when(i == 0)
    def _():
        zeros[...] = jnp.zeros(zeros.shape, F32)
        fill(wait_only=False)
        fill(wait_only=True)

    def scatter(h_ref):
        def start(r, carry):
            for k in range(TOP_K):
                _row_copy(h_ref, r, xs_ref, dest_ref[0, 0, r * TOP_K + k], sem).start()
            return carry

        def wait(r, carry):
            for _ in range(TOP_K):
                _row_copy(h_ref, 0, xs_ref, 0, sem).wait()
            return carry

        lax.fori_loop(0, tm, start, 0, unroll=ROW_COPY_UNROLL)
        lax.fori_loop(0, tm, wait, 0, unroll=ROW_COPY_UNROLL)

    first = 0
    for h_ref, n_tiles in zip(h_refs, tiles):
        pl.when((i >= first) & (i < first + n_tiles))(functools.partial(scatter, h_ref))
        first += n_tiles


def _dispatch(dest, hs, tm, n_rows, fill_rows):
    d = hs[0].shape[1]
    tiles = tuple(h.shape[0] // tm for h in hs)
    starts = [sum(tiles[:g]) for g in range(len(tiles))]
    h_specs = [pl.BlockSpec((tm, d), lambda i, lo=lo, nt=nt: (jnp.clip(i - lo, 0, nt - 1), 0))
               for lo, nt in zip(starts, tiles)]
    return pl.pallas_call(
        functools.partial(_dispatch_body, tm=tm, tiles=tiles),
        grid=(sum(tiles),),
        in_specs=[pl.BlockSpec((1, 1, tm * TOP_K), lambda i: (i, 0, 0), memory_space=pltpu.SMEM),
                  pl.BlockSpec(memory_space=pltpu.SMEM)] + h_specs,
        out_specs=pl.BlockSpec(memory_space=pl.ANY),
        out_shape=jax.ShapeDtypeStruct((n_rows, d), F32),
        scratch_shapes=[pltpu.VMEM((MOE_ROWS, d), F32), pltpu.SemaphoreType.DMA(()), pltpu.SemaphoreType.DMA(())],
        compiler_params=pltpu.CompilerParams(
            dimension_semantics=("arbitrary",), vmem_limit_bytes=VMEM_LIMIT),
        name="dispatch_rows",
    )(dest.reshape(sum(tiles), 1, tm * TOP_K), fill_rows, *hs)


def _combine_body(dest_ref, gate_ref, x_ref, g_ref, ys_ref, o_ref, ybuf, sems, *, tm, apply_norm):
    half = tm // 2

    def gather(part, wait_only):
        def one_row(r, carry):
            for k in range(TOP_K):
                if wait_only:
                    _row_copy(ys_ref, 0, ybuf.at[k], 0, sems.at[part]).wait()
                else:
                    _row_copy(ys_ref, dest_ref[0, 0, r * TOP_K + k], ybuf.at[k], r, sems.at[part]).start()
            return carry
        lax.fori_loop(part * half, (part + 1) * half, one_row, 0, unroll=ROW_COPY_UNROLL)

    gather(0, wait_only=False)
    gather(1, wait_only=False)
    for part in range(2):
        gather(part, wait_only=True)
        rows = slice(part * half, (part + 1) * half)
        gate = gate_ref[rows, :]
        y = gate[:, 0:1] * ybuf[0, rows, :]
        for k in range(1, TOP_K):
            y = y + gate[:, k:k + 1] * ybuf[k, rows, :]
        x = x_ref[rows, :] + y
        if apply_norm:
            x = x * lax.rsqrt(jnp.mean(x * x, axis=-1, keepdims=True) + EPS) * g_ref[...]
        o_ref[rows, :] = x


def _combine(dest, gate, x1, gain, ys, tm, *, apply_norm):
    n, d = x1.shape
    return pl.pallas_call(
        functools.partial(_combine_body, tm=tm, apply_norm=apply_norm),
        grid=(n // tm,),
        in_specs=[pl.BlockSpec((1, 1, tm * TOP_K), lambda i: (i, 0, 0), memory_space=pltpu.SMEM),
                  pl.BlockSpec((tm, TOP_K), lambda i: (i, 0)),
                  pl.BlockSpec((tm, d), lambda i: (i, 0)),
                  pl.BlockSpec((1, d), lambda i: (0, 0)),
                  pl.BlockSpec(memory_space=pl.ANY)],
        out_specs=pl.BlockSpec((tm, d), lambda i: (i, 0)),
        out_shape=jax.ShapeDtypeStruct((n, d), F32),
        scratch_shapes=[pltpu.VMEM((TOP_K, tm, d), F32), pltpu.SemaphoreType.DMA((2,))],
        compiler_params=pltpu.CompilerParams(
            dimension_semantics=("arbitrary",), vmem_limit_bytes=VMEM_LIMIT),
        name="combine_final_norm",
    )(dest.reshape(n // tm, 1, tm * TOP_K), gate, x1, gain, ys)


def _token_tile(n, candidates):
    for c in candidates:
        if n % c == 0:
            return c
    raise ValueError(f"token count {n} has no tile in {candidates}")


def _lane_vector(values, first_lane):
    out = jnp.zeros((1, HEAD_DIM), F32)
    return lax.dynamic_update_slice(out, values.reshape(1, -1).astype(F32), (0, first_lane))


def _route(top_idx, n_exp):
    n_tok = top_idx.shape[0]
    n_assign = n_tok * TOP_K
    e_flat = top_idx.reshape(-1)
    onehot = (e_flat[:, None] == jnp.arange(n_exp)[None, :]).astype(jnp.int32)
    seen = jnp.cumsum(onehot, axis=0)
    rank = jnp.sum(onehot * seen, axis=1) - 1
    counts = seen[-1]
    padded = ((counts + MOE_ROWS - 1) // MOE_ROWS) * MOE_ROWS
    pend = jnp.cumsum(padded)
    pstart = pend - padded
    dest = (pstart[e_flat] + rank).astype(jnp.int32).reshape(n_tok, TOP_K)
    n_blocks = -(-n_assign // MOE_ROWS) + n_exp
    block_start = jnp.arange(n_blocks, dtype=pend.dtype) * MOE_ROWS
    block_expert = jnp.minimum(
        jnp.sum(pend[None, :] <= block_start[:, None], axis=1), n_exp - 1).astype(jnp.int32)
    n_used = (pend[-1] // MOE_ROWS).astype(jnp.int32).reshape(1)
    n_rows = n_blocks * MOE_ROWS
    fill_rows = jnp.stack([jnp.append(pstart + counts, pend[-1]), jnp.append(pend, n_rows)]).astype(jnp.int32)
    return dest, fill_rows, block_expert, n_used, n_rows


def kernel(x_prompt, x_sample, state_conv_a, state_delta, state_hgrn, norm_mix, w_in, conv_a, a_log, dt_bias,
           norm_a, lb_logits, norm_b, w_branch, w_out, norm_ffn, w_router, b_router, w_gate_up, b_gate_up,
           w_down, b_down, norm_final):
    bp, tp, d = x_prompt.shape
    bs, ts, _ = x_sample.shape
    depth = w_in.shape[0]
    n_exp = w_router.shape[-1]
    assert d == D_HEADS and tp % BLOCK_ROWS == 0 and ts <= BLOCK_ROWS
    assert min(tp, ts) >= CONV_WIDTH - 1
    d_conv = 3 * D_HEADS
    tail = CONV_WIDTH - 1

    groups = [
        dict(x=x_prompt.reshape(bp * tp, d), batch=bp, t=tp, tin=BLOCK_ROWS, first=True),
        dict(x=x_sample.reshape(bs * ts, d), batch=bs, t=ts, tin=ts, first=False),
    ]
    for g in groups:
        g['n'] = g['batch'] * g['t']
        g['tm_in'] = _token_tile(g['n'], (1024, 768, 512, 256))
        g['tm'] = _token_tile(g['n'], (256,))
        g['conv'], g['delta'], g['hgrn'] = [], [], []

    for l in range(depth):
        w_l = w_in[l]
        w_main = jnp.concatenate([w_l[:, :d_conv], w_l[:, d_conv + 2 * N_HEADS:]], axis=1).astype(BF16)
        w_ba = jnp.pad(w_l[:, d_conv:d_conv + 2 * N_HEADS], ((0, 0), (0, HEAD_DIM - 2 * N_HEADS))).astype(BF16)
        gain = norm_mix[l].reshape(1, d)
        conv_w = jnp.pad(conv_a[l], ((0, CONV_PAD - CONV_WIDTH), (0, 0)))
        alog_l = _lane_vector(a_log[l], N_HEADS)
        dtb_l = _lane_vector(dt_bias[l], N_HEADS)
        na = norm_a[l].reshape(1, HEAD_DIM)
        nb = norm_b[l].reshape(1, HEAD_DIM)
        lbl = jnp.pad(lb_logits.astype(F32), ((0, CONV_PAD - lb_logits.shape[0]), (0, 0)), constant_values=-1e30)
        w_r = jnp.pad(w_router[l], ((0, 0), (0, HEAD_DIM - n_exp)))
        w_r_hi = w_r.astype(BF16)
        w_r = jnp.concatenate([w_r_hi, (w_r - w_r_hi.astype(F32)).astype(BF16), w_r_hi], axis=0)
        b_r = jnp.pad(b_router[l], (0, HEAD_DIM - n_exp), constant_values=ROUTER_PAD).reshape(1, HEAD_DIM)
        wb0, wb1, wo = w_branch[l, 0].astype(BF16), w_branch[l, 1].astype(BF16), w_out[l].astype(BF16)

        for g in groups:
            batch, t, tin = g['batch'], g['t'], g['tin']
            nt = t // tin
            proj = _inproj(g['x'], gain, w_main, g['tm_in'], D_HEADS)
            ba = _inproj(g['x'], gain, w_ba, g['tm_in'], HEAD_DIM)
            if g['first']:
                cs = jnp.zeros((batch, CONV_PAD, d_conv), F32)
                s_delta = jnp.zeros((batch, N_HEADS, HEAD_DIM, HEAD_DIM), F32)
                s_hgrn_t = s_delta
            else:
                cs = jnp.pad(state_conv_a[l], ((0, 0), (CONV_PAD - tail, 0), (0, 0)))
                s_delta = state_delta[l]
                s_hgrn_t = jnp.swapaxes(state_hgrn[l], -1, -2)
            o_a, sd = _gdn(proj, ba, cs, s_delta, conv_w, alog_l, dtb_l, na, batch=batch, tin=tin, nt=nt)
            o_b, sh = _gla(proj, lbl, nb, s_hgrn_t, batch=batch, tin=tin, nt=nt, layer=l)
            g['conv'].append(jnp.stack([proj[(b + 1) * t - tail:(b + 1) * t, :d_conv] for b in range(batch)]))
            g['delta'].append(sd)
            g['hgrn'].append(jnp.swapaxes(sh, -1, -2))
            g['x1'], g['h'], g['gates'], g['top_idx'] = _merge(o_a, o_b, proj, g['x'], wb0, wb1, wo,
                                                  norm_ffn[l].reshape(1, d), w_r, b_r, g['tm'])

        top_idx = jnp.concatenate([g['top_idx'][:, :TOP_K] for g in groups], axis=0)
        dest, fill_rows, block_expert, n_used, n_rows = _route(top_idx, n_exp)
        row = 0
        for g in groups:
            g['dest'] = dest[row:row + g['n']]
            g['gate'] = g['gates'][:, :TOP_K]
            row += g['n']
        tm_rows = groups[0]['tm']
        assert all(g['tm'] == tm_rows for g in groups)
        xs = _dispatch(dest, [g['h'] for g in groups], tm_rows, n_rows, fill_rows)
        ys = _moe(block_expert, n_used, xs, w_gate_up[l].astype(BF16), b_gate_up[l][:, None, :],
                  w_down[l].astype(BF16), b_down[l][:, None, :])
        for g in groups:
            g['x'] = _combine(g['dest'], g['gate'], g['x1'], norm_final.reshape(1, d), ys, g['tm'],
                              apply_norm=l + 1 == depth)

    gp, gs = groups
    return (gp['x'].reshape(bp, tp, d), gs['x'].reshape(bs, ts, d),
            jnp.stack(gp['conv']), jnp.stack(gp['delta']), jnp.stack(gp['hgrn']),
            jnp.stack(gs['conv']), jnp.stack(gs['delta']), jnp.stack(gs['hgrn']))
```

```python
import functools

import jax
import jax.numpy as jnp
from jax import lax
from jax.experimental import pallas as pl
from jax.experimental.pallas import tpu as pltpu

F32 = jnp.float32
BF16 = jnp.bfloat16

EPS = 1e-6
HEAD_DIM = 128
N_HEADS = 16
D_HEADS = N_HEADS * HEAD_DIM
CONV_WIDTH = 4
CONV_PAD = 8
BLOCK_ROWS = 256
GLA_CHUNK = 64
GLA_SUB = 8
HEAD_GROUP = 4
GDN_HEAD_GROUP = 4
DIAG_UNROLL = True
TOP_K = 4
ROUTER_PAD = -1e30
SWIGLU_LIMIT = 7.0
SWIGLU_ALPHA = 1.702
MOE_ROWS = 512
MOE_FF_TILE = 1024
ROW_COPY_UNROLL = 2
VMEM_LIMIT = 56 * 1024 * 1024

NT_DIMS = (((1,), (1,)), ((), ()))
TN_DIMS = (((0,), (0,)), ((), ()))


def _sigmoid(x):
    return 1.0 / (1.0 + jnp.exp(-x))


def _bdot(a, b):
    return jnp.dot(a.astype(BF16), b.astype(BF16), preferred_element_type=F32)


def _bdot_general(a, b, dims):
    return lax.dot_general(a.astype(BF16), b.astype(BF16), dims, preferred_element_type=F32)


def _split3(x):
    hi = x.astype(BF16)
    r1 = x - hi.astype(F32)
    mid = r1.astype(BF16)
    lo = (r1 - mid.astype(F32)).astype(BF16)
    return hi, mid, lo


def _dot_x01(x, m01):
    m = m01.astype(BF16)
    return jnp.dot(jnp.concatenate(_split3(x), axis=1), jnp.concatenate([m, m, m], axis=0),
                   preferred_element_type=F32)


def _dot_01x(m01, x):
    m = m01.astype(BF16)
    return jnp.dot(jnp.concatenate([m, m, m], axis=1), jnp.concatenate(_split3(x), axis=0),
                   preferred_element_type=F32)


def _dot_01xt(m01, x):
    m = m01.astype(BF16)
    return lax.dot_general(jnp.concatenate([m, m, m], axis=1), jnp.concatenate(_split3(x), axis=1), NT_DIMS,
                           preferred_element_type=F32)


def _trace_round_robin(gens):
    results = [None] * len(gens)
    live = list(range(len(gens)))
    while live:
        for i in list(live):
            try:
                next(gens[i])
            except StopIteration as stop:
                results[i] = stop.value
                live.remove(i)
    return results


def _inproj_body(x_ref, g_ref, w_ref, o_ref, xn_ref):
    @pl.when(pl.program_id(1) == 0)
    def _():
        x = x_ref[...]
        ms = jnp.mean(x * x, axis=-1, keepdims=True)
        xn_ref[...] = (x * lax.rsqrt(ms + EPS) * g_ref[...]).astype(BF16)

    o_ref[...] = jnp.dot(xn_ref[...], w_ref[...], preferred_element_type=F32)


def _inproj(x, gain, w, tm, tn):
    n, d = x.shape
    nc = w.shape[1]
    return pl.pallas_call(
        _inproj_body,
        grid=(n // tm, nc // tn),
        in_specs=[pl.BlockSpec((tm, d), lambda i, j: (i, 0)),
                  pl.BlockSpec((1, d), lambda i, j: (0, 0)),
                  pl.BlockSpec((d, tn), lambda i, j: (0, j))],
        out_specs=pl.BlockSpec((tm, tn), lambda i, j: (i, j)),
        out_shape=jax.ShapeDtypeStruct((n, nc), F32),
        scratch_shapes=[pltpu.VMEM((tm, d), BF16)],
        compiler_params=pltpu.CompilerParams(
            dimension_semantics=("parallel", "arbitrary"), vmem_limit_bytes=VMEM_LIMIT),
        name="inproj",
    )(x, gain, w)


def _compute_rows(tin):
    for rows in (HEAD_DIM, BLOCK_ROWS):
        if tin <= rows:
            return rows
    raise ValueError(f"block of {tin} tokens exceeds {BLOCK_ROWS}")


def _pad_rows(x, rows):
    if x.shape[0] == rows:
        return x
    return jnp.concatenate([x, jnp.zeros((rows - x.shape[0],) + x.shape[1:], x.dtype)], axis=0)


def _gdn_body(qkv_ref, z_ref, ba_ref, cs_ref, s0_ref, cw_ref, alog_ref, dtb_ref, na_ref,
              o_ref, sout_ref, xbuf, s_scr, *, tin, nt, rows):
    R = rows
    t = pl.program_id(1)

    @pl.when(t == 0)
    def _():
        xbuf[0:CONV_PAD, :] = cs_ref[0]
        s_scr[...] = s0_ref[0]

    if nt > 1:
        @pl.when(t > 0)
        def _():
            xbuf[0:CONV_PAD, :] = xbuf[tin:tin + CONV_PAD, :]

    xbuf[CONV_PAD:CONV_PAD + tin, :] = qkv_ref[...]
    if tin < R:
        xbuf[CONV_PAD + tin:CONV_PAD + R, :] = jnp.zeros((R - tin, xbuf.shape[1]), F32)

    rows = lax.broadcasted_iota(jnp.int32, (R, HEAD_DIM), 0)
    lanes = lax.broadcasted_iota(jnp.int32, (R, HEAD_DIM), 1)
    valid = rows < tin
    ti = lax.broadcasted_iota(jnp.int32, (R, R), 0)
    si = lax.broadcasted_iota(jnp.int32, (R, R), 1)
    causal = ti >= si
    merge_key = jnp.where(ti > si, ti ^ si, 0)

    ba = _pad_rows(ba_ref[...], R)
    sp_in = ba + dtb_ref[...]
    softplus = jnp.maximum(sp_in, 0.0) + jnp.log(1.0 + jnp.exp(-jnp.abs(sp_in)))
    g_all = jnp.where(valid, -jnp.exp(alog_ref[...]) * softplus, 0.0)
    beta_all = jnp.where(valid, _sigmoid(ba), 0.0)
    gcum_all = _dot_01x(causal, g_all)
    gate_src = jnp.where(lanes < N_HEADS, beta_all, gcum_all)
    DIAG = HEAD_DIM
    assert R in (DIAG, 2 * DIAG)
    diag_key = merge_key[:DIAG, :DIAG]

    sel_l = lax.broadcasted_iota(jnp.int32, (HEAD_DIM, 2 * HEAD_DIM), 0)
    sel_c = lax.broadcasted_iota(jnp.int32, (HEAD_DIM, 2 * HEAD_DIM), 1)
    lane0 = (lanes == 0).astype(F32)

    def head(h, s_old):
        off = h * HEAD_DIM

        def conv_silu(base):
            cols = pl.ds(pl.multiple_of(base + off, HEAD_DIM), HEAD_DIM)
            acc = None
            for j in range(CONV_WIDTH):
                term = xbuf[pl.ds(CONV_PAD - (CONV_WIDTH - 1) + j, R), cols] * cw_ref[j:j + 1, cols]
                acc = term if acc is None else acc + term
            return acc * _sigmoid(acc)

        qc = conv_silu(0)
        kc = conv_silu(D_HEADS)
        vc = conv_silu(2 * D_HEADS)
        qn = qc * lax.rsqrt(jnp.sum(qc * qc, axis=-1, keepdims=True) + EPS) * (HEAD_DIM ** -0.5)
        kn = jnp.where(valid, kc * lax.rsqrt(jnp.sum(kc * kc, axis=-1, keepdims=True) + EPS), 0.0)

        sel = (((sel_c < HEAD_DIM) & (sel_l == h)) |
               ((sel_c >= HEAD_DIM) & (sel_l == h + N_HEADS))).astype(F32)
        bg = _dot_x01(gate_src, sel)
        yield
        beta = bg[:, :HEAD_DIM]
        gc = bg[:, HEAD_DIM:]
        gc_last = gc[R - 1:R, :]
        gc_t = jnp.concatenate([gc] * (R // HEAD_DIM), axis=1)
        gc_s = _dot_01xt(lane0, gc)
        yield
        decay = jnp.where(causal, jnp.exp(jnp.where(causal, gc_t - gc_s, 0.0)), 0.0)

        kb = kn * beta
        aq = _bdot_general(jnp.concatenate([kb, qn], axis=0), kn, NT_DIMS)
        yield
        a_mat = jnp.where(ti > si, aq[:R] * decay, 0.0)
        qk = aq[R:] * decay

        a_diag = [a_mat[i * DIAG:(i + 1) * DIAG, i * DIAG:(i + 1) * DIAG] for i in range(R // DIAG)]
        n_diag = None
        b = 1
        while b < DIAG:
            level = (diag_key >= b) & (diag_key < 2 * b)
            c_lvl = [jnp.where(level, a, 0.0) for a in a_diag]
            if n_diag is None:
                n_diag = [-c for c in c_lvl]
            else:
                p_lvl = [c + _bdot(n, c) for n, c in zip(n_diag, c_lvl)]
                yield
                n_diag = [n - p - _bdot(p, n) for n, p in zip(n_diag, p_lvl)]
                yield
            b *= 2

        rhs = jnp.concatenate([vc * beta, kb * jnp.exp(gc)], axis=1)
        if len(a_diag) == 1:
            sol = rhs + _bdot(n_diag[0], rhs)
        else:
            n_top, n_bot = n_diag
            a_cross = a_mat[DIAG:, :DIAG]
            p_cross = a_cross + _bdot(n_bot, a_cross)
            yield
            n_cross = -(p_cross + _bdot(p_cross, n_top))
            yield
            rhs_top, rhs_bot = rhs[:DIAG], rhs[DIAG:]
            sol = jnp.concatenate([rhs_top + _bdot(n_top, rhs_top),
                                   rhs_bot + _bdot(n_cross, rhs_top) + _bdot(n_bot, rhs_bot)], axis=0)
        yield
        u = sol[:, :HEAD_DIM]
        w = sol[:, HEAD_DIM:]

        qd = qn * jnp.exp(gc)
        kd = kn * jnp.exp(gc_last - gc)
        ws = _bdot(jnp.concatenate([w, qd], axis=0), s_old)
        yield
        v_new = u - ws[:R]
        o = ws[R:] + _bdot(qk, v_new)
        s_new = s_old * jnp.exp(gc_last) + _bdot_general(kd, v_new, TN_DIMS)
        yield

        cols = pl.ds(pl.multiple_of(off, HEAD_DIM), HEAD_DIM)
        on = o * lax.rsqrt(jnp.mean(o * o, axis=-1, keepdims=True) + EPS) * na_ref[...]
        z = z_ref[:, cols]
        o_ref[:, cols] = (on[:tin] * (z * _sigmoid(z))).astype(BF16)
        return s_new

    def head_group(hg, carry):
        heads = [hg * GDN_HEAD_GROUP + u for u in range(GDN_HEAD_GROUP)]
        states = _trace_round_robin([head(h, s_scr[h]) for h in heads])
        for h, s in zip(heads, states):
            s_scr[h] = s
        return carry

    lax.fori_loop(0, N_HEADS // GDN_HEAD_GROUP, head_group, 0)

    @pl.when(t == nt - 1)
    def _():
        sout_ref[0] = s_scr[...]


def _gdn(proj, ba, conv_state, s0, conv_w, alog_l, dtb_l, norm_a, *, batch, tin, nt):
    rows = _compute_rows(tin)
    body = functools.partial(_gdn_body, tin=tin, nt=nt, rows=rows)
    return pl.pallas_call(
        body,
        grid=(batch, nt),
        in_specs=[
            pl.BlockSpec((tin, 3 * D_HEADS), lambda b, t: (b * nt + t, 0)),
            pl.BlockSpec((tin, D_HEADS), lambda b, t: (b * nt + t, 3)),
            pl.BlockSpec((tin, HEAD_DIM), lambda b, t: (b * nt + t, 0)),
            pl.BlockSpec((1, CONV_PAD, 3 * D_HEADS), lambda b, t: (b, 0, 0)),
            pl.BlockSpec((1, N_HEADS, HEAD_DIM, HEAD_DIM), lambda b, t: (b, 0, 0, 0)),
            pl.BlockSpec((CONV_PAD, 3 * D_HEADS), lambda b, t: (0, 0)),
            pl.BlockSpec((1, HEAD_DIM), lambda b, t: (0, 0)),
            pl.BlockSpec((1, HEAD_DIM), lambda b, t: (0, 0)),
            pl.BlockSpec((1, HEAD_DIM), lambda b, t: (0, 0)),
        ],
        out_specs=[
            pl.BlockSpec((tin, D_HEADS), lambda b, t: (b * nt + t, 0)),
            pl.BlockSpec((1, N_HEADS, HEAD_DIM, HEAD_DIM), lambda b, t: (b, 0, 0, 0)),
        ],
        out_shape=[jax.ShapeDtypeStruct((batch * nt * tin, D_HEADS), BF16),
                   jax.ShapeDtypeStruct((batch, N_HEADS, HEAD_DIM, HEAD_DIM), F32)],
        scratch_shapes=[pltpu.VMEM((rows + CONV_PAD, 3 * D_HEADS), F32),
                        pltpu.VMEM((N_HEADS, HEAD_DIM, HEAD_DIM), F32)],
        compiler_params=pltpu.CompilerParams(
            dimension_semantics=("parallel", "arbitrary"), vmem_limit_bytes=VMEM_LIMIT),
        name="gated_delta",
    )(proj, proj, ba, conv_state, s0, conv_w, alog_l, dtb_l, norm_a)


def _gla_body(q_ref, f_ref, i_ref, z_ref, lbl_ref, nb_ref, s0_ref, o_ref, sout_ref,
              s_scr, *row_scr, tin, nt, layer, rows):
    R = rows
    n_chunks = R // GLA_CHUNK
    t = pl.program_id(1)

    @pl.when(t == 0)
    def _():
        s_scr[...] = s0_ref[0]

    rows = lax.broadcasted_iota(jnp.int32, (R, HEAD_DIM), 0)
    valid = rows < tin
    ti = lax.broadcasted_iota(jnp.int32, (R, R), 0)
    si = lax.broadcasted_iota(jnp.int32, (R, R), 1)
    chunk_tri = ((ti >= si) & ((ti // GLA_CHUNK) == (si // GLA_CHUNK))).astype(F32)
    pair_key = jnp.where(ti > si, ti ^ si, 0)
    row_local = lax.broadcasted_iota(jnp.int32, (GLA_SUB, HEAD_DIM), 0)

    def head_pre(h, slot):
        q_scr, k_scr, v_scr, b_scr, _ = (r.at[slot] for r in row_scr)
        cols = pl.ds(pl.multiple_of(h * HEAD_DIM, HEAD_DIM), HEAD_DIM)
        lbl = lbl_ref[:, cols]
        lb_e = jnp.exp(lbl - jnp.max(lbl, axis=0, keepdims=True))
        lb = jnp.sum(lb_e[0:layer + 1], axis=0, keepdims=True) / jnp.sum(lb_e, axis=0, keepdims=True)

        fl = _pad_rows(f_ref[:, cols], R)
        qx = _pad_rows(q_ref[:, cols], R)
        vv = _pad_rows(i_ref[:, cols], R)
        log_f = jnp.where(valid, jnp.log(lb + (1.0 - lb) * _sigmoid(fl)), 0.0)
        kk = jnp.where(valid, (1.0 - lb) * _sigmoid(-fl), 0.0)
        qq = qx * _sigmoid(qx)
        bc = _dot_01x(chunk_tri, log_f)
        yield
        q_scr[...] = qq
        k_scr[...] = kk
        v_scr[...] = vv
        b_scr[...] = bc

        def block_row(period, row):
            return jnp.concatenate(
                [jnp.broadcast_to(b_scr[start + row:start + row + 1, :], (period, HEAD_DIM))
                 for start in range(0, R, period)], axis=0)

        b_last = block_row(GLA_CHUNK, GLA_CHUNK - 1)
        qd = qq * jnp.exp(bc)
        kd = kk * jnp.exp(b_last - bc)

        scores = jnp.zeros((R, R), F32)
        hs = GLA_SUB
        while hs < GLA_CHUNK:
            b_mid = block_row(2 * hs, hs - 1)
            lower = (rows & hs) != 0
            q_fac = jnp.where(lower, qq * jnp.exp(jnp.minimum(bc - b_mid, 0.0)), 0.0)
            k_fac = jnp.where(lower, 0.0, kk * jnp.exp(jnp.minimum(b_mid - bc, 0.0)))
            level = _bdot_general(q_fac, k_fac, NT_DIMS)
            yield
            scores = jnp.where((pair_key >= hs) & (pair_key < 2 * hs), level, scores)
            hs *= 2
        o_off = _bdot(scores, vv)
        yield
        return qd, kd, vv, o_off

    def diag_block(m, carry):
        r0 = pl.multiple_of(m * GLA_SUB, GLA_SUB)
        for slot in range(HEAD_GROUP):
            q_scr, k_scr, v_scr, b_scr, od_scr = (r.at[slot] for r in row_scr)
            q_blk = q_scr[pl.ds(r0, GLA_SUB), :]
            b_blk = b_scr[pl.ds(r0, GLA_SUB), :]
            acc = jnp.zeros((GLA_SUB, HEAD_DIM), F32)
            for s in range(GLA_SUB):
                b_s = b_scr[pl.ds(r0 + s, 1), :]
                k_s = k_scr[pl.ds(r0 + s, 1), :]
                v_s = v_scr[pl.ds(r0 + s, 1), :]
                e = jnp.where(row_local >= s, jnp.exp(jnp.minimum(b_blk - b_s, 0.0)), 0.0)
                col = jnp.sum(q_blk * k_s * e, axis=-1, keepdims=True)
                acc = acc + col * v_s
            od_scr[pl.ds(r0, GLA_SUB), :] = acc
        return carry

    def head_post(h, slot, pre, s_t):
        b_scr, od_scr = row_scr[3].at[slot], row_scr[4].at[slot]
        qd, kd, vv, o_off = pre
        o_inter = []
        for c in range(n_chunks):
            sl = slice(c * GLA_CHUNK, (c + 1) * GLA_CHUNK)
            o_inter.append(_bdot_general(qd[sl], s_t, NT_DIMS))
            f_last = jnp.exp(b_scr[(c + 1) * GLA_CHUNK - 1:(c + 1) * GLA_CHUNK, :])
            s_t = s_t * f_last + _bdot_general(vv[sl], kd[sl], TN_DIMS)
            yield

        o = jnp.concatenate(o_inter, axis=0) + o_off + od_scr[...]
        on = o * lax.rsqrt(jnp.mean(o * o, axis=-1, keepdims=True) + EPS) * nb_ref[...]
        cols = pl.ds(pl.multiple_of(h * HEAD_DIM, HEAD_DIM), HEAD_DIM)
        z = z_ref[:, cols]
        o_ref[:, cols] = (on[:tin] * _sigmoid(z)).astype(BF16)
        return s_t

    def head_group(hg, carry):
        heads = [hg * HEAD_GROUP + u for u in range(HEAD_GROUP)]
        states = [s_scr[h] for h in heads]
        pre = _trace_round_robin([head_pre(h, u) for u, h in enumerate(heads)])
        lax.fori_loop(0, R // GLA_SUB, diag_block, 0, unroll=DIAG_UNROLL)
        states = _trace_round_robin([head_post(h, u, p, s) for u, (h, p, s) in enumerate(zip(heads, pre, states))])
        for h, s in zip(heads, states):
            s_scr[h] = s
        return carry

    lax.fori_loop(0, N_HEADS // HEAD_GROUP, head_group, 0)

    @pl.when(t == nt - 1)
    def _():
        sout_ref[0] = s_scr[...]


def _gla(proj, lb_logits, norm_b, s0_t, *, batch, tin, nt, layer):
    rows = _compute_rows(tin)
    body = functools.partial(_gla_body, tin=tin, nt=nt, layer=layer, rows=rows)
    col_spec = lambda c: pl.BlockSpec((tin, D_HEADS), lambda b, t: (b * nt + t, c))
    head_rows = pltpu.VMEM((HEAD_GROUP, rows, HEAD_DIM), F32)
    return pl.pallas_call(
        body,
        grid=(batch, nt),
        in_specs=[
            col_spec(4), col_spec(5), col_spec(6), col_spec(7),
            pl.BlockSpec(lb_logits.shape, lambda b, t: (0, 0)),
            pl.BlockSpec((1, HEAD_DIM), lambda b, t: (0, 0)),
            pl.BlockSpec((1, N_HEADS, HEAD_DIM, HEAD_DIM), lambda b, t: (b, 0, 0, 0)),
        ],
        out_specs=[
            pl.BlockSpec((tin, D_HEADS), lambda b, t: (b * nt + t, 0)),
            pl.BlockSpec((1, N_HEADS, HEAD_DIM, HEAD_DIM), lambda b, t: (b, 0, 0, 0)),
        ],
        out_shape=[jax.ShapeDtypeStruct((batch * nt * tin, D_HEADS), BF16),
                   jax.ShapeDtypeStruct((batch, N_HEADS, HEAD_DIM, HEAD_DIM), F32)],
        scratch_shapes=[pltpu.VMEM((N_HEADS, HEAD_DIM, HEAD_DIM), F32),
                        head_rows, head_rows, head_rows, head_rows, head_rows],
        compiler_params=pltpu.CompilerParams(
            dimension_semantics=("parallel", "arbitrary"), vmem_limit_bytes=VMEM_LIMIT),
        name="hgrn2",
    )(proj, proj, proj, proj, lb_logits, norm_b, s0_t)


def _merge_body(oa_ref, ob_ref, g0_ref, g1_ref, x_ref, wb0_ref, wb1_ref, wo_ref, nf_ref, wr_ref, br_ref,
                x1_ref, h_ref, gate_ref, idx_ref):
    ua = jnp.dot(oa_ref[...], wb0_ref[...], preferred_element_type=F32)
    ub = jnp.dot(ob_ref[...], wb1_ref[...], preferred_element_type=F32)
    merged = _sigmoid(g0_ref[...]) * ua + _sigmoid(g1_ref[...]) * ub
    x1 = x_ref[...] + jnp.dot(merged.astype(BF16), wo_ref[...], preferred_element_type=F32)
    x1_ref[...] = x1
    hn = x1 * lax.rsqrt(jnp.mean(x1 * x1, axis=-1, keepdims=True) + EPS) * nf_ref[...]
    h_ref[...] = hn
    h_hi = hn.astype(BF16)
    h_lo = (hn - h_hi.astype(F32)).astype(BF16)
    logits = jnp.dot(jnp.concatenate([h_hi, h_hi, h_lo], axis=1), wr_ref[...],
                     preferred_element_type=F32) + br_ref[...]

    lane = lax.broadcasted_iota(jnp.int32, logits.shape, 1)
    gates = jnp.zeros(logits.shape, F32)
    picks = jnp.zeros(logits.shape, jnp.int32)
    top = None
    for k in range(TOP_K):
        best = jnp.max(logits, axis=-1, keepdims=True)
        pick = jnp.min(jnp.where(logits == best, lane, HEAD_DIM), axis=-1, keepdims=True)
        top = best if top is None else top
        gates = jnp.where(lane == k, jnp.exp(best - top), gates)
        picks = jnp.where(lane == k, pick, picks)
        logits = jnp.where(lane == pick, ROUTER_PAD, logits)
    gate_ref[...] = gates / jnp.sum(gates, axis=-1, keepdims=True)
    idx_ref[...] = picks


def _merge(o_a, o_b, proj, x, wb0, wb1, wo, norm_ffn, w_router, b_router, tm):
    n, d = x.shape
    const = lambda shape: pl.BlockSpec(shape, lambda i: (0, 0), pipeline_mode=pl.Buffered(1))
    return pl.pallas_call(
        _merge_body,
        grid=(n // tm,),
        in_specs=[
            pl.BlockSpec((tm, D_HEADS), lambda i: (i, 0)),
            pl.BlockSpec((tm, D_HEADS), lambda i: (i, 0)),
            pl.BlockSpec((tm, d), lambda i: (i, 8)),
            pl.BlockSpec((tm, d), lambda i: (i, 9)),
            pl.BlockSpec((tm, d), lambda i: (i, 0)),
            const(wb0.shape), const(wb1.shape), const(wo.shape),
            const((1, d)), const(w_router.shape), const((1, HEAD_DIM)),
        ],
        out_specs=[pl.BlockSpec((tm, d), lambda i: (i, 0)),
                   pl.BlockSpec((tm, d), lambda i: (i, 0)),
                   pl.BlockSpec((tm, HEAD_DIM), lambda i: (i, 0)),
                   pl.BlockSpec((tm, HEAD_DIM), lambda i: (i, 0))],
        out_shape=[jax.ShapeDtypeStruct((n, d), F32),
                   jax.ShapeDtypeStruct((n, d), F32),
                   jax.ShapeDtypeStruct((n, HEAD_DIM), F32),
                   jax.ShapeDtypeStruct((n, HEAD_DIM), jnp.int32)],
        compiler_params=pltpu.CompilerParams(
            dimension_semantics=("parallel",), vmem_limit_bytes=VMEM_LIMIT),
        name="merge_out",
    )(o_a, o_b, proj, proj, x, wb0, wb1, wo, norm_ffn, w_router, b_router)


def _moe_body(be_ref, nu_ref, x_ref, wg_ref, wu_ref, bg_ref, bu_ref, wd_ref, bd_ref, o_ref):
    i = pl.program_id(0)
    f = pl.program_id(1)

    @pl.when(i < nu_ref[0])
    def _():
        x = x_ref[...].astype(BF16)
        gate = jnp.dot(x, wg_ref[0], preferred_element_type=F32) + bg_ref[0]
        up = jnp.dot(x, wu_ref[0], preferred_element_type=F32) + bu_ref[0]
        gate = jnp.minimum(gate, SWIGLU_LIMIT)
        up = jnp.clip(up, -SWIGLU_LIMIT, SWIGLU_LIMIT)
        act = (up + 1.0) * gate * _sigmoid(SWIGLU_ALPHA * gate)
        part = jnp.dot(act.astype(BF16), wd_ref[0], preferred_element_type=F32)

        @pl.when(f == 0)
        def _():
            o_ref[...] = part + bd_ref[0]

        @pl.when(f > 0)
        def _():
            o_ref[...] += part

    @pl.when((i >= nu_ref[0]) & (f == 0))
    def _():
        o_ref[...] = jnp.zeros(o_ref.shape, F32)


def _moe(block_expert, n_used, xs, w_gu, b_gu, w_d, b_d):
    n_rows, d = xs.shape
    n_exp, _, two_ff = w_gu.shape
    d_ff = two_ff // 2
    tf = min(MOE_FF_TILE, d_ff)
    nf = d_ff // tf
    nblk = n_rows // MOE_ROWS
    grid_spec = pltpu.PrefetchScalarGridSpec(
        num_scalar_prefetch=2,
        grid=(nblk, nf),
        in_specs=[
            pl.BlockSpec((MOE_ROWS, d), lambda i, f, be, nu: (i, 0)),
            pl.BlockSpec((1, d, tf), lambda i, f, be, nu: (be[i], 0, f)),
            pl.BlockSpec((1, d, tf), lambda i, f, be, nu: (be[i], 0, nf + f)),
            pl.BlockSpec((1, 1, tf), lambda i, f, be, nu: (be[i], 0, f)),
            pl.BlockSpec((1, 1, tf), lambda i, f, be, nu: (be[i], 0, nf + f)),
            pl.BlockSpec((1, tf, d), lambda i, f, be, nu: (be[i], f, 0)),
            pl.BlockSpec((1, 1, d), lambda i, f, be, nu: (be[i], 0, 0)),
        ],
        out_specs=pl.BlockSpec((MOE_ROWS, d), lambda i, f, be, nu: (i, 0)),
    )
    return pl.pallas_call(
        _moe_body,
        grid_spec=grid_spec,
        out_shape=jax.ShapeDtypeStruct((n_rows, d), F32),
        compiler_params=pltpu.CompilerParams(
            dimension_semantics=("arbitrary", "arbitrary"), vmem_limit_bytes=VMEM_LIMIT),
        name="expert_mlp",
    )(block_expert, n_used, xs, w_gu, w_gu, b_gu, b_gu, w_d, b_d)


def _row_copy(src_ref, src_row, dst_ref, dst_row, sem):
    return pltpu.make_async_copy(src_ref.at[pl.ds(src_row, 1)], dst_ref.at[pl.ds(dst_row, 1)], sem)


def _dispatch_body(dest_ref, fill_ref, *refs, tm, tiles):
    h_refs = refs[:len(tiles)]
    xs_ref, zeros, sem, zero_sem = refs[len(tiles):]
    i = pl.program_id(0)
    n_exp = fill_ref.shape[1] - 1

    def fill(wait_only):
        def act(copy):
            copy.wait() if wait_only else copy.start()

        def one_expert(e, carry):
            def one_row(r, c):
                act(_row_copy(zeros, 0, xs_ref, r, zero_sem))
                return c
            lax.fori_loop(fill_ref[0, e], fill_ref[1, e], one_row, 0)
            return carry

        def one_block(j, carry):
            rows = pl.ds(pl.multiple_of(j * MOE_ROWS, MOE_ROWS), MOE_ROWS)
            act(pltpu.make_async_copy(zeros, xs_ref.at[rows], sem))
            return carry

        lax.fori_loop(0, n_exp, one_expert, 0)
        lax.fori_loop(fill_ref[0, n_exp] // MOE_ROWS, fill_ref[1, n_exp] // MOE_ROWS, one_block, 0)

    @pl.when(i == 0)
    def _():
        zeros[...] = jnp.zeros(zeros.shape, F32)
        fill(wait_only=False)
        fill(wait_only=True)

    def scatter(h_ref):
        def start(r, carry):
            for k in range(TOP_K):
                _row_copy(h_ref, r, xs_ref, dest_ref[0, 0, r * TOP_K + k], sem).start()
            return carry

        def wait(r, carry):
            for _ in range(TOP_K):
                _row_copy(h_ref, 0, xs_ref, 0, sem).wait()
            return carry

        lax.fori_loop(0, tm, start, 0, unroll=ROW_COPY_UNROLL)
        lax.fori_loop(0, tm, wait, 0, unroll=ROW_COPY_UNROLL)

    first = 0
    for h_ref, n_tiles in zip(h_refs, tiles):
        pl.when((i >= first) & (i < first + n_tiles))(functools.partial(scatter, h_ref))
        first += n_tiles


def _dispatch(dest, hs, tm, n_rows, fill_rows):
    d = hs[0].shape[1]
    tiles = tuple(h.shape[0] // tm for h in hs)
    starts = [sum(tiles[:g]) for g in range(len(tiles))]
    h_specs = [pl.BlockSpec((tm, d), lambda i, lo=lo, nt=nt: (jnp.clip(i - lo, 0, nt - 1), 0))
               for lo, nt in zip(starts, tiles)]
    return pl.pallas_call(
        functools.partial(_dispatch_body, tm=tm, tiles=tiles),
        grid=(sum(tiles),),
        in_specs=[pl.BlockSpec((1, 1, tm * TOP_K), lambda i: (i, 0, 0), memory_space=pltpu.SMEM),
                  pl.BlockSpec(memory_space=pltpu.SMEM)] + h_specs,
        out_specs=pl.BlockSpec(memory_space=pl.ANY),
        out_shape=jax.ShapeDtypeStruct((n_rows, d), F32),
        scratch_shapes=[pltpu.VMEM((MOE_ROWS, d), F32), pltpu.SemaphoreType.DMA(()), pltpu.SemaphoreType.DMA(())],
        compiler_params=pltpu.CompilerParams(
            dimension_semantics=("arbitrary",), vmem_limit_bytes=VMEM_LIMIT),
        name="dispatch_rows",
    )(dest.reshape(sum(tiles), 1, tm * TOP_K), fill_rows, *hs)


def _combine_body(dest_ref, gate_ref, x_ref, g_ref, ys_ref, o_ref, ybuf, sems, *, tm, apply_norm):
    half = tm // 2

    def gather(part, wait_only):
        def one_row(r, carry):
            for k in range(TOP_K):
                if wait_only:
                    _row_copy(ys_ref, 0, ybuf.at[k], 0, sems.at[part]).wait()
                else:
                    _row_copy(ys_ref, dest_ref[0, 0, r * TOP_K + k], ybuf.at[k], r, sems.at[part]).start()
            return carry
        lax.fori_loop(part * half, (part + 1) * half, one_row, 0, unroll=ROW_COPY_UNROLL)

    gather(0, wait_only=False)
    gather(1, wait_only=False)
    for part in range(2):
        gather(part, wait_only=True)
        rows = slice(part * half, (part + 1) * half)
        gate = gate_ref[rows, :]
        y = gate[:, 0:1] * ybuf[0, rows, :]
        for k in range(1, TOP_K):
            y = y + gate[:, k:k + 1] * ybuf[k, rows, :]
        x = x_ref[rows, :] + y
        if apply_norm:
            x = x * lax.rsqrt(jnp.mean(x * x, axis=-1, keepdims=True) + EPS) * g_ref[...]
        o_ref[rows, :] = x


def _combine(dest, gate, x1, gain, ys, tm, *, apply_norm):
    n, d = x1.shape
    return pl.pallas_call(
        functools.partial(_combine_body, tm=tm, apply_norm=apply_norm),
        grid=(n // tm,),
        in_specs=[pl.BlockSpec((1, 1, tm * TOP_K), lambda i: (i, 0, 0), memory_space=pltpu.SMEM),
                  pl.BlockSpec((tm, TOP_K), lambda i: (i, 0)),
                  pl.BlockSpec((tm, d), lambda i: (i, 0)),
                  pl.BlockSpec((1, d), lambda i: (0, 0)),
                  pl.BlockSpec(memory_space=pl.ANY)],
        out_specs=pl.BlockSpec((tm, d), lambda i: (i, 0)),
        out_shape=jax.ShapeDtypeStruct((n, d), F32),
        scratch_shapes=[pltpu.VMEM((TOP_K, tm, d), F32), pltpu.SemaphoreType.DMA((2,))],
        compiler_params=pltpu.CompilerParams(
            dimension_semantics=("arbitrary",), vmem_limit_bytes=VMEM_LIMIT),
        name="combine_final_norm",
    )(dest.reshape(n // tm, 1, tm * TOP_K), gate, x1, gain, ys)


def _token_tile(n, candidates):
    for c in candidates:
        if n % c == 0:
            return c
    raise ValueError(f"token count {n} has no tile in {candidates}")


def _lane_vector(values, first_lane):
    out = jnp.zeros((1, HEAD_DIM), F32)
    return lax.dynamic_update_slice(out, values.reshape(1, -1).astype(F32), (0, first_lane))


def _route(top_idx, n_exp):
    n_tok = top_idx.shape[0]
    n_assign = n_tok * TOP_K
    e_flat = top_idx.reshape(-1)
    onehot = (e_flat[:, None] == jnp.arange(n_exp)[None, :]).astype(jnp.int32)
    seen = jnp.cumsum(onehot, axis=0)
    rank = jnp.sum(onehot * seen, axis=1) - 1
    counts = seen[-1]
    padded = ((counts + MOE_ROWS - 1) // MOE_ROWS) * MOE_ROWS
    pend = jnp.cumsum(padded)
    pstart = pend - padded
    dest = (pstart[e_flat] + rank).astype(jnp.int32).reshape(n_tok, TOP_K)
    n_blocks = -(-n_assign // MOE_ROWS) + n_exp
    block_start = jnp.arange(n_blocks, dtype=pend.dtype) * MOE_ROWS
    block_expert = jnp.minimum(
        jnp.sum(pend[None, :] <= block_start[:, None], axis=1), n_exp - 1).astype(jnp.int32)
    n_used = (pend[-1] // MOE_ROWS).astype(jnp.int32).reshape(1)
    n_rows = n_blocks * MOE_ROWS
    fill_rows = jnp.stack([jnp.append(pstart + counts, pend[-1]), jnp.append(pend, n_rows)]).astype(jnp.int32)
    return dest, fill_rows, block_expert, n_used, n_rows


def kernel(x_prompt, x_sample, state_conv_a, state_delta, state_hgrn, norm_mix, w_in, conv_a, a_log, dt_bias,
           norm_a, lb_logits, norm_b, w_branch, w_out, norm_ffn, w_router, b_router, w_gate_up, b_gate_up,
           w_down, b_down, norm_final):
    bp, tp, d = x_prompt.shape
    bs, ts, _ = x_sample.shape
    depth = w_in.shape[0]
    n_exp = w_router.shape[-1]
    assert d == D_HEADS and tp % BLOCK_ROWS == 0 and ts <= BLOCK_ROWS
    assert min(tp, ts) >= CONV_WIDTH - 1
    d_conv = 3 * D_HEADS
    tail = CONV_WIDTH - 1

    groups = [
        dict(x=x_prompt.reshape(bp * tp, d), batch=bp, t=tp, tin=BLOCK_ROWS, first=True),
        dict(x=x_sample.reshape(bs * ts, d), batch=bs, t=ts, tin=ts, first=False),
    ]
    for g in groups:
        g['n'] = g['batch'] * g['t']
        g['tm_in'] = _token_tile(g['n'], (1024, 768, 512, 256))
        g['tm'] = _token_tile(g['n'], (256,))
        g['conv'], g['delta'], g['hgrn'] = [], [], []

    for l in range(depth):
        w_l = w_in[l]
        w_main = jnp.concatenate([w_l[:, :d_conv], w_l[:, d_conv + 2 * N_HEADS:]], axis=1).astype(BF16)
        w_ba = jnp.pad(w_l[:, d_conv:d_conv + 2 * N_HEADS], ((0, 0), (0, HEAD_DIM - 2 * N_HEADS))).astype(BF16)
        gain = norm_mix[l].reshape(1, d)
        conv_w = jnp.pad(conv_a[l], ((0, CONV_PAD - CONV_WIDTH), (0, 0)))
        alog_l = _lane_vector(a_log[l], N_HEADS)
        dtb_l = _lane_vector(dt_bias[l], N_HEADS)
        na = norm_a[l].reshape(1, HEAD_DIM)
        nb = norm_b[l].reshape(1, HEAD_DIM)
        lbl = jnp.pad(lb_logits.astype(F32), ((0, CONV_PAD - lb_logits.shape[0]), (0, 0)), constant_values=-1e30)
        w_r = jnp.pad(w_router[l], ((0, 0), (0, HEAD_DIM - n_exp)))
        w_r_hi = w_r.astype(BF16)
        w_r = jnp.concatenate([w_r_hi, (w_r - w_r_hi.astype(F32)).astype(BF16), w_r_hi], axis=0)
        b_r = jnp.pad(b_router[l], (0, HEAD_DIM - n_exp), constant_values=ROUTER_PAD).reshape(1, HEAD_DIM)
        wb0, wb1, wo = w_branch[l, 0].astype(BF16), w_branch[l, 1].astype(BF16), w_out[l].astype(BF16)

        for g in groups:
            batch, t, tin = g['batch'], g['t'], g['tin']
            nt = t // tin
            proj = _inproj(g['x'], gain, w_main, g['tm_in'], D_HEADS)
            ba = _inproj(g['x'], gain, w_ba, g['tm_in'], HEAD_DIM)
            if g['first']:
                cs = jnp.zeros((batch, CONV_PAD, d_conv), F32)
                s_delta = jnp.zeros((batch, N_HEADS, HEAD_DIM, HEAD_DIM), F32)
                s_hgrn_t = s_delta
            else:
                cs = jnp.pad(state_conv_a[l], ((0, 0), (CONV_PAD - tail, 0), (0, 0)))
                s_delta = state_delta[l]
                s_hgrn_t = jnp.swapaxes(state_hgrn[l], -1, -2)
            o_a, sd = _gdn(proj, ba, cs, s_delta, conv_w, alog_l, dtb_l, na, batch=batch, tin=tin, nt=nt)
            o_b, sh = _gla(proj, lbl, nb, s_hgrn_t, batch=batch, tin=tin, nt=nt, layer=l)
            g['conv'].append(jnp.stack([proj[(b + 1) * t - tail:(b + 1) * t, :d_conv] for b in range(batch)]))
            g['delta'].append(sd)
            g['hgrn'].append(jnp.swapaxes(sh, -1, -2))
            g['x1'], g['h'], g['gates'], g['top_idx'] = _merge(o_a, o_b, proj, g['x'], wb0, wb1, wo,
                                                  norm_ffn[l].reshape(1, d), w_r, b_r, g['tm'])

        top_idx = jnp.concatenate([g['top_idx'][:, :TOP_K] for g in groups], axis=0)
        dest, fill_rows, block_expert, n_used, n_rows = _route(top_idx, n_exp)
        row = 0
        for g in groups:
            g['dest'] = dest[row:row + g['n']]
            g['gate'] = g['gates'][:, :TOP_K]
            row += g['n']
        tm_rows = groups[0]['tm']
        assert all(g['tm'] == tm_rows for g in groups)
        xs = _dispatch(dest, [g['h'] for g in groups], tm_rows, n_rows, fill_rows)
        ys = _moe(block_expert, n_used, xs, w_gate_up[l].astype(BF16), b_gate_up[l][:, None, :],
                  w_down[l].astype(BF16), b_down[l][:, None, :])
        for g in groups:
            g['x'] = _combine(g['dest'], g['gate'], g['x1'], norm_final.reshape(1, d), ys, g['tm'],
                              apply_norm=l + 1 == depth)

    gp, gs = groups
    return (gp['x'].reshape(bp, tp, d), gs['x'].reshape(bs, ts, d),
            jnp.stack(gp['conv']), jnp.stack(gp['delta']), jnp.stack(gp['hgrn']),
            jnp.stack(gs['conv']), jnp.stack(gs['delta']), jnp.stack(gs['hgrn']))
```
